```python
import math
import jax
import jax.numpy as jnp
from jax import lax
import numpy as np

D_MODEL = 4096
BATCH = 4
SEQ = 2048
DEPTH = 2
DEC_BATCH = 128
DEC_SEQ = 1
PAST_LEN = 16384
PAGE_SIZE = 128

ML_HEADS = 8
ML_DK = 128
ML_DV = 256
ML_CHUNK = 64
MLA_HEADS = 16
MLA_NOPE = 128
MLA_ROPE = 64
MLA_DV = 128
MLA_Q_LORA = 768
MLA_KV_LORA = 256
ROPE_THETA = 10000.0
DIFF_HEADS = 16
DIFF_DH = 64
REL_BUCKETS = 32
REL_MAX_DIST = 128
D_FF = 11008
N_EXPERTS = 8
TOP_K = 2
D_FF_EXPERT = 2048
Q_BLOCK = 128
NORM_EPS = 1e-6
N_BRANCH = 3
ML_W = ML_HEADS * ML_DV
MLA_W = MLA_HEADS * MLA_DV
DIFF_W = DIFF_HEADS * 2 * DIFF_DH
MLA_SCALE = (MLA_NOPE + MLA_ROPE) ** -0.5
DIFF_SCALE = DIFF_DH ** -0.5
IN_SPLITS = (
    ML_HEADS * ML_DK,
    ML_HEADS * ML_DK,
    ML_W,
    ML_W,
    ML_HEADS,
    ML_HEADS,
    MLA_Q_LORA,
    MLA_KV_LORA,
    MLA_ROPE,
    DIFF_HEADS * 2 * DIFF_DH,
    2 * DIFF_DH,
    2 * DIFF_DH,
    N_BRANCH * D_MODEL,
)
IN_OFFSETS = tuple(int(v) for v in np.cumsum(IN_SPLITS)[:-1])
IN_COLS = int(sum(IN_SPLITS))
N_DENSE = (DEPTH + 1) // 2
N_MOE = DEPTH // 2

kernel_name = 'hybrid_mlstm_mla_diffattn_gated_decoder_step'


def rms_norm(x, g):
    xf = x.astype(jnp.float32)
    y = xf * lax.rsqrt(jnp.mean(xf * xf, axis=-1, keepdims=True) + NORM_EPS)
    return (y * g.astype(jnp.float32)).astype(x.dtype)


def rope_tables(pos):
    inv = ROPE_THETA ** (-jnp.arange(0, MLA_ROPE, 2, dtype=jnp.float32) / MLA_ROPE)
    ang = pos.astype(jnp.float32)[:, None] * inv[None, :]
    return jnp.cos(ang), jnp.sin(ang)


def apply_rope(x, cos, sin):
    xf = x.astype(jnp.float32)
    x1, x2 = jnp.split(xf, 2, axis=-1)
    return jnp.concatenate([x1 * cos - x2 * sin, x1 * sin + x2 * cos], axis=-1).astype(x.dtype)


def rel_bucket(dist):
    n = jnp.maximum(dist, 0)
    max_exact = REL_BUCKETS // 2
    nf = jnp.maximum(n, max_exact).astype(jnp.float32)
    large = max_exact + (jnp.log(nf / max_exact) / math.log(REL_MAX_DIST / max_exact)
                         * (REL_BUCKETS - max_exact)).astype(jnp.int32)
    large = jnp.minimum(large, REL_BUCKETS - 1)
    return jnp.where(n < max_exact, n, large)


def rel_bias(table, q_pos, k_pos):
    b = rel_bucket(q_pos[:, None] - k_pos[None, :])
    return jnp.transpose(table[b].astype(jnp.float32), (2, 0, 1))


def mlstm_chunk(carry, inp):
    c_prev, n_prev, m_prev = carry
    q, k, v, ig, lf = inp
    L = q.shape[1]
    b = jnp.cumsum(lf, axis=1)
    causal = jnp.tril(jnp.ones((L, L), dtype=bool))
    d = b[:, :, None, :] - b[:, None, :, :] + ig[:, None, :, :]
    d = jnp.where(causal[None, :, :, None], d, -jnp.inf)
    inter = b + m_prev[:, None, :]
    m_t = jnp.maximum(inter, jnp.max(d, axis=2))
    s = jnp.einsum('bthd,bshd->btsh', q, k) * jnp.exp(d - m_t[:, :, None, :])
    w_inter = jnp.exp(inter - m_t)
    num = (jnp.einsum('btsh,bshv->bthv', s, v)
           + w_inter[..., None] * jnp.einsum('bthd,bhdv->bthv', q, c_prev))
    den = jnp.sum(s, axis=2) + w_inter * jnp.einsum('bthd,bhd->bth', q, n_prev)
    h = num / jnp.maximum(jnp.abs(den), jnp.exp(-m_t))[..., None]
    b_last = b[:, -1, :]
    g = b_last[:, None, :] - b + ig
    m_new = jnp.maximum(b_last + m_prev, jnp.max(g, axis=1))
    w_state = jnp.exp(g - m_new[:, None, :])
    decay = jnp.exp(b_last + m_prev - m_new)
    c_new = decay[..., None, None] * c_prev + jnp.einsum('bsh,bshd,bshv->bhdv', w_state, k, v)
    n_new = decay[..., None] * n_prev + jnp.einsum('bsh,bshd->bhd', w_state, k)
    return (c_new, n_new, m_new), h


def mlstm_scan(q, k, v, ig, lf, state0, chunk):
    B, S = q.shape[:2]
    nc = S // chunk

    def blocks(a):
        return jnp.moveaxis(a.reshape(B, nc, chunk, *a.shape[2:]), 1, 0)

    state, h = lax.scan(mlstm_chunk, state0, tuple(blocks(a) for a in (q, k, v, ig, lf)))
    return jnp.moveaxis(h, 0, 1).reshape(B, S, ML_HEADS, ML_DV), state


def mla_attend(q_lat, q_rope, c, kr, mask):
    s = (jnp.einsum('bqhc,bkc->bhqk', q_lat, c)
         + jnp.einsum('bqhr,bkr->bhqk', q_rope, kr)).astype(jnp.float32) * MLA_SCALE
    p = jax.nn.softmax(jnp.where(mask, s, -jnp.inf), axis=-1)
    return jnp.einsum('bhqk,bkc->bqhc', p.astype(c.dtype), c)


def diff_attend(q, k, v, bias, lam, mask):
    s = jnp.einsum('bqhjd,bkjd->jbhqk', q, k).astype(jnp.float32) * DIFF_SCALE + bias[None, None]
    p = jax.nn.softmax(jnp.where(mask, s, -jnp.inf), axis=-1)
    a = p[0] - lam * p[1]
    return jnp.einsum('bhqk,bkv->bqhv', a.astype(v.dtype), v)


def map_query_blocks(block_fn, qs):
    B, S = qs[0].shape[:2]
    nb = S // Q_BLOCK
    blocked = tuple(jnp.moveaxis(a.reshape(B, nb, Q_BLOCK, *a.shape[2:]), 1, 0) for a in qs)
    out = lax.map(lambda args: block_fn(args[0], *args[1:]), (jnp.arange(nb, dtype=jnp.int32),) + blocked)
    return jnp.moveaxis(out, 0, 1).reshape(B, S, *out.shape[3:])


def make_prompt_attn(rel_table):
    def block_pos(i, n_keys):
        qpos = i * Q_BLOCK + jnp.arange(Q_BLOCK, dtype=jnp.int32)
        kpos = jnp.arange(n_keys, dtype=jnp.int32)
        return qpos, kpos, kpos[None, :] <= qpos[:, None]

    def mla_fn(q_lat, q_rope, c, kr):
        def block(i, ql, qr):
            _, _, mask = block_pos(i, c.shape[1])
            return mla_attend(ql, qr, c, kr, mask)
        return map_query_blocks(block, (q_lat, q_rope))

    def diff_fn(q, k, v, lam):
        def block(i, qb):
            qpos, kpos, mask = block_pos(i, k.shape[1])
            return diff_attend(qb, k, v, rel_bias(rel_table, qpos, kpos), lam, mask)
        return map_query_blocks(block, (q,))

    return mla_fn, diff_fn


def make_sample_attn(layer, cache_mla_c, cache_mla_kr, cache_diff_k, cache_diff_v, page_table, rel_table):
    kpos = jnp.arange(PAST_LEN + DEC_SEQ, dtype=jnp.int32)
    qpos = PAST_LEN + jnp.arange(DEC_SEQ, dtype=jnp.int32)
    mask = kpos[None, :] <= qpos[:, None]
    bias = rel_bias(rel_table, qpos, kpos)

    def with_past(cache, pt, new):
        rows = cache[layer, pt].reshape(PAST_LEN, *new.shape[1:]).astype(new.dtype)
        return jnp.concatenate([rows, new], axis=0)

    def mla_fn(q_lat, q_rope, c, kr):
        def one(args):
            pt, ql, qr, cn, krn = args
            return mla_attend(ql[None], qr[None], with_past(cache_mla_c, pt, cn)[None],
                              with_past(cache_mla_kr, pt, krn)[None], mask)[0]
        return lax.map(one, (page_table, q_lat, q_rope, c, kr))

    def diff_fn(q, k, v, lam):
        def one(args):
            pt, qb, kn, vn = args
            return diff_attend(qb[None], with_past(cache_diff_k, pt, kn)[None],
                               with_past(cache_diff_v, pt, vn)[None], bias, lam, mask)[0]
        return lax.map(one, (page_table, q, k, v))

    return mla_fn, diff_fn


def token_mixer(h, pos, layer, ml_state, ml_chunk, mla_attn, diff_attn, mw):
    (w_in, ml_gate_bias, ml_out_norm, mla_q_norm, mla_w_uq, mla_q_gain, mla_kv_norm, mla_kr_gain,
     mla_w_uk, mla_w_uv, diff_q_gain, diff_k_gain, diff_lambda, diff_out_norm,
     w_ml_o, w_mla_o, w_diff_o, w_out) = mw
    f32 = jnp.float32
    B, S, _ = h.shape
    (ml_q, ml_k, ml_v, ml_o, ml_i, ml_f, mla_qa, mla_ckv, mla_kr,
     d_q, d_k, d_v, gate_pre) = jnp.split(h @ w_in, IN_OFFSETS, axis=-1)

    q = ml_q.reshape(B, S, ML_HEADS, ML_DK).astype(f32)
    k = ml_k.reshape(B, S, ML_HEADS, ML_DK).astype(f32) * (ML_DK ** -0.5)
    v = ml_v.reshape(B, S, ML_HEADS, ML_DV).astype(f32)
    ig = ml_i.astype(f32) + ml_gate_bias[0].astype(f32)
    lf = jax.nn.log_sigmoid(ml_f.astype(f32) + ml_gate_bias[1].astype(f32))
    h_ml, (ml_c, ml_n, ml_m) = mlstm_scan(q, k, v, ig, lf, ml_state, ml_chunk)
    o_gate = jax.nn.sigmoid(ml_o.astype(f32)).reshape(B, S, ML_HEADS, ML_DV)
    h_ml = (rms_norm(h_ml, ml_out_norm) * o_gate).reshape(B, S, ML_W).astype(h.dtype)

    cos, sin = rope_tables(pos)
    qf = (rms_norm(mla_qa, mla_q_norm) @ mla_w_uq).reshape(B, S, MLA_HEADS, MLA_NOPE + MLA_ROPE)
    qf = rms_norm(qf, mla_q_gain)
    q_nope = qf[..., :MLA_NOPE]
    q_rope = apply_rope(qf[..., MLA_NOPE:], cos[None, :, None, :], sin[None, :, None, :])
    c_kv = rms_norm(mla_ckv, mla_kv_norm)
    k_rope = apply_rope(rms_norm(mla_kr, mla_kr_gain), cos[None], sin[None])
    q_lat = jnp.einsum('bshn,chn->bshc', q_nope, mla_w_uk)
    o_lat = mla_attn(q_lat, q_rope, c_kv, k_rope)
    h_mla = jnp.einsum('bshc,chv->bshv', o_lat, mla_w_uv).reshape(B, S, MLA_W)

    dq = rms_norm(d_q.reshape(B, S, DIFF_HEADS, 2, DIFF_DH), diff_q_gain)
    dk = rms_norm(d_k.reshape(B, S, 2, DIFF_DH), diff_k_gain)
    lam_init = 0.8 - 0.6 * math.exp(-0.3 * layer)
    lq1, lk1, lq2, lk2 = diff_lambda.astype(f32)
    lam = jnp.exp(jnp.sum(lq1 * lk1)) - jnp.exp(jnp.sum(lq2 * lk2)) + lam_init
    o_d = diff_attn(dq, dk, d_v, lam)
    h_diff = (rms_norm(o_d, diff_out_norm) * (1.0 - lam_init)).reshape(B, S, DIFF_W)

    g = jax.nn.sigmoid(gate_pre.astype(f32)).reshape(B, S, N_BRANCH, D_MODEL)
    merged = (g[:, :, 0] * (h_ml @ w_ml_o).astype(f32)
              + g[:, :, 1] * (h_mla @ w_mla_o).astype(f32)
              + g[:, :, 2] * (h_diff @ w_diff_o).astype(f32))
    y = merged.astype(h.dtype) @ w_out
    new_state = (c_kv, k_rope, dk.reshape(B, S, 2 * DIFF_DH), d_v, ml_c, ml_n, ml_m)
    return y, new_state


def swiglu(h, wg, wu, wd):
    return (jax.nn.silu(h @ wg) * (h @ wu)) @ wd


def moe_swiglu(h, router, wg, wu, wd):
    logits = (h @ router).astype(jnp.float32)
    top_val, top_idx = lax.top_k(logits, TOP_K)
    top_w = jax.nn.softmax(top_val, axis=-1)
    gate = jnp.sum(jax.nn.one_hot(top_idx, N_EXPERTS, dtype=jnp.float32) * top_w[..., None], axis=-2)
    out = jnp.zeros(h.shape, jnp.float32)
    for e in range(N_EXPERTS):
        out = out + gate[..., e:e + 1] * swiglu(h, wg[e], wu[e], wd[e]).astype(jnp.float32)
    return out.astype(h.dtype)


def channel_mixer(h, layer, ffn_w_gate, ffn_w_up, ffn_w_down, moe_router, moe_w_gate, moe_w_up, moe_w_down):
    i = layer // 2
    if layer % 2 == 0:
        return swiglu(h, ffn_w_gate[i], ffn_w_up[i], ffn_w_down[i])
    return moe_swiglu(h, moe_router[i], moe_w_gate[i], moe_w_up[i], moe_w_down[i])


def setup_inputs(seed: int = 0) -> dict:
    key = jax.random.key(seed)
    keys = iter(list(jax.random.split(key, 64)))
    f32 = jnp.float32

    def nrm(shape, scale=1.0):
        return jax.random.normal(next(keys), shape, f32) * scale

    def gain(shape):
        return 1.0 + 0.1 * jax.random.normal(next(keys), shape, f32)

    d = D_MODEL
    n_pages = PAST_LEN // PAGE_SIZE
    n_used = DEC_BATCH * n_pages
    n_pool = n_used + max(1, n_used // 4)
    page_table = jax.random.permutation(next(keys), n_pool)[:n_used].reshape(DEC_BATCH, n_pages).astype(jnp.int32)
    return {
        'x_prompt': nrm((BATCH, SEQ, d)),
        'x_sample': nrm((DEC_BATCH, DEC_SEQ, d)),
        'cache_mla_c': nrm((DEPTH, n_pool, PAGE_SIZE, MLA_KV_LORA)),
        'cache_mla_kr': nrm((DEPTH, n_pool, PAGE_SIZE, MLA_ROPE)),
        'cache_diff_k': nrm((DEPTH, n_pool, PAGE_SIZE, 2 * DIFF_DH)),
        'cache_diff_v': nrm((DEPTH, n_pool, PAGE_SIZE, 2 * DIFF_DH)),
        'state_mlstm_c': nrm((DEPTH, DEC_BATCH, ML_HEADS, ML_DK, ML_DV), 0.5),
        'state_mlstm_n': nrm((DEPTH, DEC_BATCH, ML_HEADS, ML_DK), 0.5),
        'state_mlstm_m': nrm((DEPTH, DEC_BATCH, ML_HEADS), 0.5),
        'page_table': page_table,
        'rel_bias_table': nrm((REL_BUCKETS, DIFF_HEADS), 0.5),
        'norm_mix': gain((DEPTH, d)),
        'norm_ffn': gain((DEPTH, d)),
        'w_in': nrm((DEPTH, d, IN_COLS), d ** -0.5),
        'ml_gate_bias': jnp.stack([nrm((DEPTH, ML_HEADS), 0.1), 3.0 + nrm((DEPTH, ML_HEADS), 0.5)], axis=1),
        'ml_out_norm': gain((DEPTH, ML_HEADS, ML_DV)),
        'mla_q_norm': gain((DEPTH, MLA_Q_LORA)),
        'mla_w_uq': nrm((DEPTH, MLA_Q_LORA, MLA_HEADS * (MLA_NOPE + MLA_ROPE)), MLA_Q_LORA ** -0.5),
        'mla_q_gain': gain((DEPTH, MLA_NOPE + MLA_ROPE)),
        'mla_kv_norm': gain((DEPTH, MLA_KV_LORA)),
        'mla_kr_gain': gain((DEPTH, MLA_ROPE)),
        'mla_w_uk': nrm((DEPTH, MLA_KV_LORA, MLA_HEADS, MLA_NOPE), MLA_KV_LORA ** -0.5),
        'mla_w_uv': nrm((DEPTH, MLA_KV_LORA, MLA_HEADS, MLA_DV), MLA_KV_LORA ** -0.5),
        'diff_q_gain': gain((DEPTH, DIFF_DH)),
        'diff_k_gain': gain((DEPTH, DIFF_DH)),
        'diff_lambda': nrm((DEPTH, 4, DIFF_DH), 0.1),
        'diff_out_norm': gain((DEPTH, 2 * DIFF_DH)),
        'w_ml_o': nrm((DEPTH, ML_W, d), ML_W ** -0.5),
        'w_mla_o': nrm((DEPTH, MLA_W, d), MLA_W ** -0.5),
        'w_diff_o': nrm((DEPTH, DIFF_W, d), DIFF_W ** -0.5),
        'w_out': nrm((DEPTH, d, d), d ** -0.5),
        'ffn_w_gate': nrm((N_DENSE, d, D_FF), d ** -0.5),
        'ffn_w_up': nrm((N_DENSE, d, D_FF), d ** -0.5),
        'ffn_w_down': nrm((N_DENSE, D_FF, d), D_FF ** -0.5),
        'moe_router': nrm((N_MOE, d, N_EXPERTS), d ** -0.5),
        'moe_w_gate': nrm((N_MOE, N_EXPERTS, d, D_FF_EXPERT), d ** -0.5),
        'moe_w_up': nrm((N_MOE, N_EXPERTS, d, D_FF_EXPERT), d ** -0.5),
        'moe_w_down': nrm((N_MOE, N_EXPERTS, D_FF_EXPERT, d), D_FF_EXPERT ** -0.5),
    }


def reference(x_prompt, x_sample, cache_mla_c, cache_mla_kr, cache_diff_k, cache_diff_v,
              state_mlstm_c, state_mlstm_n, state_mlstm_m, page_table, rel_bias_table,
              norm_mix, norm_ffn, w_in, ml_gate_bias, ml_out_norm, mla_q_norm, mla_w_uq, mla_q_gain,
              mla_kv_norm, mla_kr_gain, mla_w_uk, mla_w_uv, diff_q_gain, diff_k_gain, diff_lambda,
              diff_out_norm, w_ml_o, w_mla_o, w_diff_o, w_out, ffn_w_gate, ffn_w_up, ffn_w_down,
              moe_router, moe_w_gate, moe_w_up, moe_w_down):
    f32 = jnp.float32
    pos_p = jnp.arange(SEQ, dtype=jnp.int32)
    pos_s = PAST_LEN + jnp.arange(DEC_SEQ, dtype=jnp.int32)
    mla_p, diff_p = make_prompt_attn(rel_bias_table)
    xp, xs = x_prompt, x_sample
    bp = x_prompt.shape[0]
    per_p, per_s = [], []
    for l in range(DEPTH):
        mw = (w_in[l], ml_gate_bias[l], ml_out_norm[l], mla_q_norm[l], mla_w_uq[l], mla_q_gain[l],
              mla_kv_norm[l], mla_kr_gain[l], mla_w_uk[l], mla_w_uv[l], diff_q_gain[l], diff_k_gain[l],
              diff_lambda[l], diff_out_norm[l], w_ml_o[l], w_mla_o[l], w_diff_o[l], w_out[l])
        mla_s, diff_s = make_sample_attn(l, cache_mla_c, cache_mla_kr, cache_diff_k, cache_diff_v,
                                         page_table, rel_bias_table)
        ml0_p = (jnp.zeros((bp, ML_HEADS, ML_DK, ML_DV), f32), jnp.zeros((bp, ML_HEADS, ML_DK), f32),
                 jnp.zeros((bp, ML_HEADS), f32))
        ml0_s = (state_mlstm_c[l].astype(f32), state_mlstm_n[l].astype(f32), state_mlstm_m[l].astype(f32))

        yp, st_p = token_mixer(rms_norm(xp, norm_mix[l]), pos_p, l, ml0_p, ML_CHUNK, mla_p, diff_p, mw)
        ys, st_s = token_mixer(rms_norm(xs, norm_mix[l]), pos_s, l, ml0_s, DEC_SEQ, mla_s, diff_s, mw)
        xp = xp + yp
        xs = xs + ys
        xp = xp + channel_mixer(rms_norm(xp, norm_ffn[l]), l, ffn_w_gate, ffn_w_up, ffn_w_down,
                                moe_router, moe_w_gate, moe_w_up, moe_w_down)
        xs = xs + channel_mixer(rms_norm(xs, norm_ffn[l]), l, ffn_w_gate, ffn_w_up, ffn_w_down,
                                moe_router, moe_w_gate, moe_w_up, moe_w_down)
        per_p.append(st_p)
        per_s.append(st_s)

    (p_mla_c, p_mla_kr, p_diff_k, p_diff_v, p_ml_c, p_ml_n, p_ml_m) = [
        jnp.stack([st[i] for st in per_p]) for i in range(7)]
    (s_mla_c, s_mla_kr, s_diff_k, s_diff_v, s_ml_c, s_ml_n, s_ml_m) = [
        jnp.stack([st[i] for st in per_s]) for i in range(7)]
    y_prompt = xp
    y_sample = xs
    return (y_prompt, y_sample, p_mla_c, p_mla_kr, p_diff_k, p_diff_v, p_ml_c, p_ml_n, p_ml_m,
            s_mla_c, s_mla_kr, s_diff_k, s_diff_v, s_ml_c, s_ml_n, s_ml_m)
```

```python
import functools
import math

import jax
import jax.numpy as jnp
from jax import lax
from jax.experimental import pallas as pl
from jax.experimental.pallas import tpu as pltpu

F32 = jnp.float32
BF16 = jnp.bfloat16
HIGHEST = lax.Precision.HIGHEST

LANES = 128
SUBLANES = 8
VMEM_LIMIT_BYTES = 56 * 1024 * 1024

D_MODEL = 4096
ML_HEADS, ML_DK, ML_DV = 8, 128, 256
MLA_HEADS, MLA_NOPE, MLA_ROPE, MLA_DV = 16, 128, 64, 128
MLA_Q_LORA, MLA_KV_LORA = 768, 256
ROPE_THETA = 10000.0
DIFF_HEADS, DIFF_DH = 16, 64
REL_BUCKETS, REL_MAX_DIST = 32, 128
N_EXPERTS, TOP_K = 8, 2
NORM_EPS = 1e-6
MLA_SCALE = (MLA_NOPE + MLA_ROPE) ** -0.5
DIFF_SCALE = DIFF_DH ** -0.5
ML_W = ML_HEADS * ML_DV
MLA_W = MLA_HEADS * MLA_DV
DIFF_W = DIFF_HEADS * 2 * DIFF_DH
MLA_KW = MLA_KV_LORA + 2 * MLA_ROPE

COL_GATE = 0
COL_MLQ = COL_GATE + 3 * D_MODEL
COL_MLK = COL_MLQ + ML_HEADS * ML_DK
COL_MLV = COL_MLK + ML_HEADS * ML_DK
COL_MLO = COL_MLV + ML_W
COL_DQ = COL_MLO + ML_W
COL_QA = COL_DQ + DIFF_W
COL_DK = COL_QA + MLA_Q_LORA + MLA_KV_LORA
COL_DV = COL_DK + 2 * DIFF_DH
COL_KR = COL_DV + 2 * DIFF_DH
COL_IF = COL_KR + LANES
PROJ_COLS = COL_IF + LANES

ML_CHUNK = 256
ATT_T = 256
DEC_PAGES = 16
PREP_TM = 320
SAMPLE_G = 8


def _tile(n, *candidates):
    return next(c for c in candidates if n % c == 0)


def _params(sem):
    return pltpu.CompilerParams(dimension_semantics=sem, vmem_limit_bytes=VMEM_LIMIT_BYTES)


def _dot(a, b):
    return jnp.dot(a, b, preferred_element_type=F32)


def _dot_nt(a, b, precision=None):
    return lax.dot_general(a, b, (((1,), (1,)), ((), ())), precision=precision,
                           preferred_element_type=F32)


def _rmsnorm_kernel(x_ref, g_ref, o_ref):
    x = x_ref[...]
    y = x * lax.rsqrt(jnp.mean(x * x, axis=-1, keepdims=True) + NORM_EPS)
    o_ref[...] = (y * g_ref[...]).astype(o_ref.dtype)


def rmsnorm_rows(x, g, tm, out_dtype=BF16):
    m, d = x.shape
    return pl.pallas_call(
        _rmsnorm_kernel, grid=(m // tm,),
        in_specs=[pl.BlockSpec((tm, d), lambda i: (i, 0)), pl.BlockSpec((1, d), lambda i: (0, 0))],
        out_specs=pl.BlockSpec((tm, d), lambda i: (i, 0)),
        out_shape=jax.ShapeDtypeStruct((m, d), out_dtype),
        compiler_params=_params(("parallel",)), name="rmsnorm_rows")(x, g.reshape(1, d))


def _matmul_kernel(*refs, nk, has_res):
    if has_res:
        x_ref, w_ref, r_ref, o_ref = refs[:4]
    else:
        x_ref, w_ref, o_ref = refs[:3]
        r_ref = None

    def finish(acc):
        if has_res:
            acc = acc + r_ref[...]
        o_ref[...] = acc.astype(o_ref.dtype)

    if nk == 1:
        finish(_dot(x_ref[...], w_ref[...]))
        return
    acc_ref = refs[-1]
    k = pl.program_id(2)

    @pl.when(k == 0)
    def _():
        acc_ref[...] = jnp.zeros_like(acc_ref)

    acc_ref[...] += _dot(x_ref[...], w_ref[...])

    @pl.when(k == nk - 1)
    def _():
        finish(acc_ref[...])


def matmul(x, w, tm, tn, tk, res=None, out_dtype=F32):
    m, kd = x.shape
    n = w.shape[1]
    nk = kd // tk
    in_specs = [pl.BlockSpec((tm, tk), lambda i, j, k: (i, k)),
                pl.BlockSpec((tk, tn), lambda i, j, k: (k, j))]
    args = [x, w]
    if res is not None:
        in_specs.append(pl.BlockSpec((tm, tn), lambda i, j, k: (i, j)))
        args.append(res)
    return pl.pallas_call(
        functools.partial(_matmul_kernel, nk=nk, has_res=res is not None),
        grid=(m // tm, n // tn, nk), in_specs=in_specs,
        out_specs=pl.BlockSpec((tm, tn), lambda i, j, k: (i, j)),
        out_shape=jax.ShapeDtypeStruct((m, n), out_dtype),
        scratch_shapes=[pltpu.VMEM((tm, tn), F32)] if nk > 1 else [],
        compiler_params=_params(("parallel", "parallel", "arbitrary")), name="matmul")(*args)


def _seg_rmsnorm(x, bd, gain, width):
    ssq = lax.dot_general(x * x, bd, (((1,), (0,)), ((), ())), precision=HIGHEST,
                          preferred_element_type=F32)
    return x * lax.rsqrt(ssq * (1.0 / width) + NORM_EPS) * gain


def _rope_lo64(x, cos, sin):
    swapped = pltpu.roll(x, MLA_ROPE // 2, 1) + pltpu.roll(x, LANES - MLA_ROPE // 2, 1)
    return x * cos + swapped * sin


def _prep_kernel(qa_ref, dq_ref, dk_ref, dv_ref, kr_ref, cos_ref, sin_ref,
                 qn_ref, wuq_ref, gqn_ref, gqr_ref, gkv_ref, gkr_ref, wuk_ref,
                 gdq_ref, gdk_ref, bd_ref,
                 ckv_o, kr_o, dk_o, kcat_o, dkb_o, dvb_o, qm_o, dqn_o):
    cos = cos_ref[...]
    sin = sin_ref[...]
    lane = lax.broadcasted_iota(jnp.int32, cos.shape, 1)
    lo64 = lane < MLA_ROPE

    ckv = qa_ref[:, MLA_Q_LORA:]
    ckv = ckv * lax.rsqrt(jnp.mean(ckv * ckv, axis=-1, keepdims=True) + NORM_EPS) * gkv_ref[...]
    ckv_o[...] = ckv
    kr = jnp.where(lo64, kr_ref[...], 0.0)
    kr = kr * lax.rsqrt(jnp.sum(kr * kr, axis=-1, keepdims=True) * (1.0 / MLA_ROPE) + NORM_EPS)
    kr = _rope_lo64(kr * gkr_ref[...], cos, sin)
    kr_o[...] = kr
    kcat_o[:, :MLA_KV_LORA] = ckv.astype(BF16)
    kcat_o[:, MLA_KV_LORA:] = kr.astype(BF16)

    bd = bd_ref[...]
    dk = _seg_rmsnorm(dk_ref[...], bd, gdk_ref[...], DIFF_DH)
    dk_o[...] = dk
    dkb_o[...] = dk.astype(BF16)
    dvb_o[...] = dv_ref[...].astype(BF16)
    gdq = gdq_ref[...]
    for h in range(DIFF_HEADS):
        sl = slice(h * LANES, (h + 1) * LANES)
        dqn_o[h] = (_seg_rmsnorm(dq_ref[:, sl], bd, gdq, DIFF_DH) * DIFF_SCALE).astype(BF16)

    qa = qa_ref[:, :MLA_Q_LORA]
    qa = qa * lax.rsqrt(jnp.mean(qa * qa, axis=-1, keepdims=True) + NORM_EPS) * qn_ref[...]
    qf = _dot(qa.astype(BF16), wuq_ref[...])
    gqn = gqn_ref[...]
    gqr = gqr_ref[...]
    nope_w = MLA_HEADS * MLA_NOPE
    for h in range(MLA_HEADS):
        nope = qf[:, h * MLA_NOPE:(h + 1) * MLA_NOPE]
        rope = qf[:, nope_w + h * LANES:nope_w + (h + 1) * LANES]
        ssq = (jnp.sum(nope * nope, axis=-1, keepdims=True)
               + jnp.sum(rope * rope, axis=-1, keepdims=True))
        inv = lax.rsqrt(ssq * (1.0 / (MLA_NOPE + MLA_ROPE)) + NORM_EPS)
        rope = _rope_lo64(rope * inv * gqr, cos, sin)
        q_lat = _dot((nope * inv * gqn).astype(BF16), wuk_ref[h])
        qm_o[h, :, :MLA_KV_LORA] = (q_lat * MLA_SCALE).astype(BF16)
        qm_o[h, :, MLA_KV_LORA:] = (rope * MLA_SCALE).astype(BF16)


def mixer_prep(proj, cos, sin, wp, tm):
    m = proj.shape[0]
    row = lambda w, cb: pl.BlockSpec((tm, w), lambda i, cb=cb: (i, cb))
    full = lambda a: pl.BlockSpec(a.shape, lambda i, nd=a.ndim: (0,) * nd)
    qa_w = MLA_Q_LORA + MLA_KV_LORA
    consts = [wp["mla_q_norm"], wp["w_uq"], wp["gq_nope"], wp["gq_rope"], wp["mla_kv_norm"],
              wp["mla_kr_gain"], wp["w_uk_t"], wp["diff_q_gain"], wp["diff_k_gain"], wp["seg_ones"]]
    outs = [((m, MLA_KV_LORA), F32), ((m, LANES), F32), ((m, LANES), F32),
            ((m, MLA_KW), BF16), ((m, LANES), BF16), ((m, LANES), BF16),
            ((MLA_HEADS, m, MLA_KW), BF16), ((DIFF_HEADS, m, LANES), BF16)]
    out_specs = [pl.BlockSpec((tm, s[-1]), lambda i: (i, 0)) if len(s) == 2
                 else pl.BlockSpec((s[0], tm, s[-1]), lambda i: (0, i, 0)) for s, _ in outs]
    return pl.pallas_call(
        _prep_kernel, grid=(m // tm,),
        in_specs=[row(qa_w, COL_QA // qa_w), row(DIFF_W, COL_DQ // DIFF_W), row(LANES, COL_DK // LANES),
                  row(LANES, COL_DV // LANES), row(LANES, COL_KR // LANES),
                  pl.BlockSpec((tm, LANES), lambda i: (i, 0)), pl.BlockSpec((tm, LANES), lambda i: (i, 0))]
                 + [full(a) for a in consts],
        out_specs=out_specs,
        out_shape=[jax.ShapeDtypeStruct(s, dt) for s, dt in outs],
        compiler_params=_params(("parallel",)), name="mixer_prep",
    )(proj, proj, proj, proj, proj, cos, sin, *consts)


def _log_sigmoid(x):
    return jnp.minimum(x, 0.0) - jnp.log(1.0 + jnp.exp(-jnp.abs(x)))


def _mlstm_prompt_kernel(q_ref, k_ref, v_ref, o_ref, if_ref, bias_ref, gain_ref, tri_ref,
                         h_o, c_o, n_o, m_o, c_sc, n_sc, m_sc, *, nc, chunk):
    hd = pl.program_id(1)
    ci = pl.program_id(2)

    @pl.when(ci == 0)
    def _():
        c_sc[...] = jnp.zeros_like(c_sc)
        n_sc[...] = jnp.zeros_like(n_sc)
        m_sc[...] = jnp.zeros_like(m_sc)

    g2 = if_ref[...] + bias_ref[...]
    cum = lax.dot_general(tri_ref[...], _log_sigmoid(g2), (((1,), (0,)), ((), ())), precision=HIGHEST,
                          preferred_element_type=F32)
    lane = lax.broadcasted_iota(jnp.int32, g2.shape, 1)
    ig_col = jnp.sum(jnp.where(lane == hd, g2, 0.0), axis=1, keepdims=True)
    b_col = jnp.sum(jnp.where(lane == hd + ML_HEADS, cum, 0.0), axis=1, keepdims=True)
    lane8 = lax.broadcasted_iota(jnp.int32, (SUBLANES, LANES), 1)
    ig_row = _dot_nt((lane8 == hd).astype(F32), g2, precision=HIGHEST)[0:1]
    b_row = _dot_nt((lane8 == hd + ML_HEADS).astype(F32), cum, precision=HIGHEST)[0:1]
    b_last = b_row[:, chunk - 1:chunk]

    m_prev = m_sc[:, 0:1]
    r_i = lax.broadcasted_iota(jnp.int32, (chunk, chunk), 0)
    c_i = lax.broadcasted_iota(jnp.int32, (chunk, chunk), 1)
    d = jnp.where(c_i <= r_i, b_col - b_row + ig_row, -jnp.inf)
    inter = b_col + m_prev
    m_t = jnp.maximum(inter, jnp.max(d, axis=1, keepdims=True))
    q = q_ref[...]
    qb = q.astype(BF16)
    ks = k_ref[...] * (ML_DK ** -0.5)
    v = v_ref[...].astype(BF16)
    s = _dot_nt(qb, ks.astype(BF16)) * jnp.exp(d - m_t)
    w_inter = jnp.exp(inter - m_t)
    c_prev = c_sc[...]
    n_prev = n_sc[...]
    num = _dot(s.astype(BF16), v) + w_inter * _dot(qb, c_prev.astype(BF16))
    den = jnp.sum(s, axis=1, keepdims=True) + w_inter * jnp.sum(q * n_prev, axis=1, keepdims=True)
    hh = num / jnp.maximum(jnp.abs(den), jnp.exp(-m_t))
    hn = hh * lax.rsqrt(jnp.mean(hh * hh, axis=-1, keepdims=True) + NORM_EPS) * gain_ref[0]
    h_o[...] = (hn * jax.nn.sigmoid(o_ref[...])).astype(h_o.dtype)

    g_col = b_last - b_col + ig_col
    m_new = jnp.maximum(b_last + m_prev, jnp.max(g_col, axis=0, keepdims=True))
    kw = ks * jnp.exp(g_col - m_new)
    decay = jnp.exp(b_last + m_prev - m_new)
    c_new = decay * c_prev + _dot(jnp.transpose(kw).astype(BF16), v)
    n_new = decay * n_prev + jnp.sum(kw, axis=0, keepdims=True)
    c_sc[...] = c_new
    n_sc[...] = n_new
    m_sc[...] = jnp.broadcast_to(m_new, m_sc.shape)

    @pl.when(ci == nc - 1)
    def _():
        c_o[0, 0] = c_new
        n_o[0, 0] = n_new
        m_o[0, 0] = jnp.broadcast_to(m_new, (1, LANES))


def mlstm_prompt(proj, bias_row, out_gain, nb, seq, chunk):
    nc = seq // chunk
    tri = jnp.tril(jnp.ones((chunk, chunk), F32))
    rowblk = lambda b, c: b * nc + c
    return pl.pallas_call(
        functools.partial(_mlstm_prompt_kernel, nc=nc, chunk=chunk),
        grid=(nb, ML_HEADS, nc),
        in_specs=[
            pl.BlockSpec((chunk, ML_DK), lambda b, h, c: (rowblk(b, c), COL_MLQ // ML_DK + h)),
            pl.BlockSpec((chunk, ML_DK), lambda b, h, c: (rowblk(b, c), COL_MLK // ML_DK + h)),
            pl.BlockSpec((chunk, ML_DV), lambda b, h, c: (rowblk(b, c), COL_MLV // ML_DV + h)),
            pl.BlockSpec((chunk, ML_DV), lambda b, h, c: (rowblk(b, c), COL_MLO // ML_DV + h)),
            pl.BlockSpec((chunk, LANES), lambda b, h, c: (rowblk(b, c), COL_IF // LANES)),
            pl.BlockSpec((1, LANES), lambda b, h, c: (0, 0)),
            pl.BlockSpec((1, 1, ML_DV), lambda b, h, c: (h, 0, 0)),
            pl.BlockSpec((chunk, chunk), lambda b, h, c: (0, 0)),
        ],
        out_specs=[
            pl.BlockSpec((chunk, ML_DV), lambda b, h, c: (rowblk(b, c), h)),
            pl.BlockSpec((1, 1, ML_DK, ML_DV), lambda b, h, c: (b, h, 0, 0)),
            pl.BlockSpec((1, 1, 1, ML_DK), lambda b, h, c: (b, h, 0, 0)),
            pl.BlockSpec((1, 1, 1, LANES), lambda b, h, c: (b, h, 0, 0)),
        ],
        out_shape=[jax.ShapeDtypeStruct((nb * seq, ML_W), BF16),
                   jax.ShapeDtypeStruct((nb, ML_HEADS, ML_DK, ML_DV), F32),
                   jax.ShapeDtypeStruct((nb, ML_HEADS, 1, ML_DK), F32),
                   jax.ShapeDtypeStruct((nb, ML_HEADS, 1, LANES), F32)],
        scratch_shapes=[pltpu.VMEM((ML_DK, ML_DV), F32), pltpu.VMEM((1, ML_DK), F32),
                        pltpu.VMEM((1, LANES), F32)],
        compiler_params=_params(("parallel", "parallel", "arbitrary")), name="mlstm_prompt",
    )(proj, proj, proj, proj, proj, bias_row, out_gain.reshape(ML_HEADS, 1, ML_DV), tri)


def _mlstm_sample_kernel(q_ref, k_ref, v_ref, o_ref, qt_ref, kt_ref, ig_ref, fg_ref, bias_ref,
                         gain_ref, c_ref, n_ref, m_ref, h_o, c_o, n_o, m_o, *, g_seqs):
    lane8 = lax.broadcasted_iota(jnp.int32, (1, ML_HEADS), 1)
    for g in range(g_seqs):
        ig_all = ig_ref[g] + bias_ref[0:1, :]
        lf_all = _log_sigmoid(fg_ref[g] + bias_ref[1:2, :])
        m_all = m_ref[0, g]
        m_out = jnp.zeros((1, ML_HEADS), F32)
        for h in range(ML_HEADS):
            ig = ig_all[:, h:h + 1]
            inter = lf_all[:, h:h + 1] + m_all[:, h:h + 1]
            m_t = jnp.maximum(inter, ig)
            q = q_ref[g:g + 1, h * ML_DK:(h + 1) * ML_DK]
            k = k_ref[g:g + 1, h * ML_DK:(h + 1) * ML_DK] * (ML_DK ** -0.5)
            v = v_ref[g:g + 1, h * ML_DV:(h + 1) * ML_DV]
            q_col = qt_ref[g, :, h:h + 1]
            k_col = kt_ref[g, :, h:h + 1] * (ML_DK ** -0.5)
            c_prev = c_ref[0, g, h]
            n_prev = n_ref[0, g, h:h + 1, :]
            w_new = jnp.exp(ig - m_t)
            w_old = jnp.exp(inter - m_t)
            s = jnp.sum(q * k, axis=1, keepdims=True) * w_new
            num = s * v + w_old * jnp.sum(q_col * c_prev, axis=0, keepdims=True)
            den = s + w_old * jnp.sum(q * n_prev, axis=1, keepdims=True)
            hh = num / jnp.maximum(jnp.abs(den), jnp.exp(-m_t))
            hn = hh * lax.rsqrt(jnp.mean(hh * hh, axis=-1, keepdims=True) + NORM_EPS) * gain_ref[h]
            og = o_ref[g:g + 1, h * ML_DV:(h + 1) * ML_DV]
            h_o[g:g + 1, h * ML_DV:(h + 1) * ML_DV] = hn * jax.nn.sigmoid(og)
            c_o[g, h] = w_old * c_prev + (w_new * k_col) * v
            n_o[g, h:h + 1, :] = w_old * n_prev + w_new * k
            m_out = jnp.where(lane8 == h, m_t, m_out)
        m_o[g] = m_out


def mlstm_sample(proj, row0, qt, kt, ig, fg, bias2, out_gain, state_c, state_n, state_m, layer, g_seqs):
    n = state_c.shape[1]
    rb = row0 // g_seqs
    return pl.pallas_call(
        functools.partial(_mlstm_sample_kernel, g_seqs=g_seqs),
        grid=(n // g_seqs,),
        in_specs=[
            pl.BlockSpec((g_seqs, ML_HEADS * ML_DK), lambda i: (rb + i, COL_MLQ // (ML_HEADS * ML_DK))),
            pl.BlockSpec((g_seqs, ML_HEADS * ML_DK), lambda i: (rb + i, COL_MLK // (ML_HEADS * ML_DK))),
            pl.BlockSpec((g_seqs, ML_W), lambda i: (rb + i, COL_MLV // ML_W)),
            pl.BlockSpec((g_seqs, ML_W), lambda i: (rb + i, COL_MLO // ML_W)),
            pl.BlockSpec((g_seqs, ML_DK, ML_HEADS), lambda i: (i, 0, 0)),
            pl.BlockSpec((g_seqs, ML_DK, ML_HEADS), lambda i: (i, 0, 0)),
            pl.BlockSpec((g_seqs, 1, ML_HEADS), lambda i: (i, 0, 0)),
            pl.BlockSpec((g_seqs, 1, ML_HEADS), lambda i: (i, 0, 0)),
            pl.BlockSpec((2, ML_HEADS), lambda i: (0, 0)),
            pl.BlockSpec((ML_HEADS, 1, ML_DV), lambda i: (0, 0, 0)),
            pl.BlockSpec((1, g_seqs, ML_HEADS, ML_DK, ML_DV), lambda i: (layer, i, 0, 0, 0)),
            pl.BlockSpec((1, g_seqs, ML_HEADS, ML_DK), lambda i: (layer, i, 0, 0)),
            pl.BlockSpec((1, g_seqs, 1, ML_HEADS), lambda i: (layer, i, 0, 0)),
        ],
        out_specs=[
            pl.BlockSpec((g_seqs, ML_W), lambda i: (i, 0)),
            pl.BlockSpec((g_seqs, ML_HEADS, ML_DK, ML_DV), lambda i: (i, 0, 0, 0)),
            pl.BlockSpec((g_seqs, ML_HEADS, ML_DK), lambda i: (i, 0, 0)),
            pl.BlockSpec((g_seqs, 1, ML_HEADS), lambda i: (i, 0, 0)),
        ],
        out_shape=[jax.ShapeDtypeStruct((n, ML_W), F32),
                   jax.ShapeDtypeStruct((n, ML_HEADS, ML_DK, ML_DV), F32),
                   jax.ShapeDtypeStruct((n, ML_HEADS, ML_DK), F32),
                   jax.ShapeDtypeStruct((n, 1, ML_HEADS), F32)],
        compiler_params=_params(("parallel",)), name="mlstm_sample",
    )(proj, proj, proj, proj, qt, kt, ig, fg, bias2, out_gain.reshape(ML_HEADS, 1, ML_DV),
      state_c, state_n, state_m.reshape(state_m.shape[0], n, 1, ML_HEADS))


def _mla_prompt_kernel(q_ref, k_ref, wuv_ref, o_ref, m_sc, l_sc, acc_sc, *, t):
    qi = pl.program_id(1)
    ki = pl.program_id(2)

    @pl.when(ki == 0)
    def _():
        m_sc[...] = jnp.full_like(m_sc, -jnp.inf)
        l_sc[...] = jnp.zeros_like(l_sc)
        acc_sc[...] = jnp.zeros_like(acc_sc)

    @pl.when(ki <= qi)
    def _():
        k = k_ref[...]
        v = k[:, :MLA_KV_LORA]
        r_i = lax.broadcasted_iota(jnp.int32, (t, t), 0) + qi * t
        c_i = lax.broadcasted_iota(jnp.int32, (t, t), 1) + ki * t
        visible = c_i <= r_i

        def head(h, carry):
            s = jnp.where(visible, _dot_nt(q_ref[h], k), -jnp.inf)
            m_prev = m_sc[h]
            m_new = jnp.maximum(m_prev, jnp.max(s, axis=1, keepdims=True))
            p = jnp.exp(s - m_new)
            alpha = jnp.exp(m_prev - m_new)
            l_sc[h] = alpha * l_sc[h] + jnp.sum(p, axis=1, keepdims=True)
            acc_sc[h] = alpha * acc_sc[h] + _dot(p.astype(BF16), v)
            m_sc[h] = m_new
            return carry

        lax.fori_loop(0, MLA_HEADS, head, 0)

    @pl.when(ki == qi)
    def _():
        for h in range(MLA_HEADS):
            o_lat = (acc_sc[h] / l_sc[h]).astype(BF16)
            o_ref[:, h * MLA_DV:(h + 1) * MLA_DV] = _dot(o_lat, wuv_ref[h]).astype(o_ref.dtype)


def mla_prompt(qm, kcat, w_uv, nb, seq, t):
    nt = seq // t
    return pl.pallas_call(
        functools.partial(_mla_prompt_kernel, t=t),
        grid=(nb, nt, nt),
        in_specs=[
            pl.BlockSpec((MLA_HEADS, t, MLA_KW), lambda b, qi, ki: (0, b * nt + qi, 0)),
            pl.BlockSpec((t, MLA_KW), lambda b, qi, ki: (b * nt + jnp.minimum(ki, qi), 0)),
            pl.BlockSpec((MLA_HEADS, MLA_KV_LORA, MLA_DV), lambda b, qi, ki: (0, 0, 0)),
        ],
        out_specs=pl.BlockSpec((t, MLA_W), lambda b, qi, ki: (b * nt + qi, 0)),
        out_shape=jax.ShapeDtypeStruct((nb * seq, MLA_W), BF16),
        scratch_shapes=[pltpu.VMEM((MLA_HEADS, t, 1), F32), pltpu.VMEM((MLA_HEADS, t, 1), F32),
                        pltpu.VMEM((MLA_HEADS, t, MLA_KV_LORA), F32)],
        compiler_params=_params(("parallel", "parallel", "arbitrary")), name="mla_prompt",
    )(qm, kcat, w_uv)


def _two_map_rows(q):
    lane = lax.broadcasted_iota(jnp.int32, q.shape, 1)
    zero = jnp.zeros_like(q)
    return jnp.concatenate([jnp.where(lane < DIFF_DH, q, zero), jnp.where(lane >= DIFF_DH, q, zero)], axis=0)


def _diff_prompt_kernel(far_ref, lam_ref, q_ref, k_ref, v_ref, bias_ref, gain_ref, o_ref,
                        m_sc, l_sc, acc_sc, *, t, out_scale):
    qi = pl.program_id(1)
    ki = pl.program_id(2)

    @pl.when(ki == 0)
    def _():
        m_sc[...] = jnp.full_like(m_sc, -jnp.inf)
        l_sc[...] = jnp.zeros_like(l_sc)
        acc_sc[...] = jnp.zeros_like(acc_sc)

    def sweep(kind):
        k = k_ref[...]
        v = v_ref[...]
        if kind == 0:
            r_i = lax.broadcasted_iota(jnp.int32, (2 * t, t), 0)
            r_i = jnp.where(r_i >= t, r_i - t, r_i)
            visible = lax.broadcasted_iota(jnp.int32, (2 * t, t), 1) <= r_i

        def head(h, carry):
            s = _dot_nt(_two_map_rows(q_ref[h]), k)
            if kind == 2:
                s = s + far_ref[h]
            else:
                b = bias_ref[kind, h]
                s = s + jnp.concatenate([b, b], axis=0)
            if kind == 0:
                s = jnp.where(visible, s, -jnp.inf)
            m_prev = m_sc[h]
            m_new = jnp.maximum(m_prev, jnp.max(s, axis=1, keepdims=True))
            p = jnp.exp(s - m_new)
            alpha = jnp.exp(m_prev - m_new)
            l_sc[h] = alpha * l_sc[h] + jnp.sum(p, axis=1, keepdims=True)
            acc_sc[h] = alpha * acc_sc[h] + _dot(p.astype(BF16), v)
            m_sc[h] = m_new
            return carry

        lax.fori_loop(0, DIFF_HEADS, head, 0)

    pl.when(ki == qi)(lambda: sweep(0))
    pl.when(ki == qi - 1)(lambda: sweep(1))
    pl.when(ki < qi - 1)(lambda: sweep(2))

    @pl.when(ki == qi)
    def _():
        lam = lam_ref[0]
        for h in range(DIFF_HEADS):
            o = acc_sc[h] / l_sc[h]
            o = o[:t] - lam * o[t:]
            o = o * lax.rsqrt(jnp.mean(o * o, axis=-1, keepdims=True) + NORM_EPS) * gain_ref[...]
            o_ref[:, h * LANES:(h + 1) * LANES] = (o * out_scale).astype(o_ref.dtype)


def diff_prompt(dqn, dkb, dvb, bias_tiles, far, lam, out_gain, out_scale, nb, seq, t):
    nt = seq // t
    smem = pl.BlockSpec(memory_space=pltpu.SMEM)
    return pl.pallas_call(
        functools.partial(_diff_prompt_kernel, t=t, out_scale=out_scale),
        grid=(nb, nt, nt),
        in_specs=[
            smem, smem,
            pl.BlockSpec((DIFF_HEADS, t, LANES), lambda b, qi, ki: (0, b * nt + qi, 0)),
            pl.BlockSpec((t, LANES), lambda b, qi, ki: (b * nt + jnp.minimum(ki, qi), 0)),
            pl.BlockSpec((t, LANES), lambda b, qi, ki: (b * nt + jnp.minimum(ki, qi), 0)),
            pl.BlockSpec((2, DIFF_HEADS, t, t), lambda b, qi, ki: (0, 0, 0, 0)),
            pl.BlockSpec((1, LANES), lambda b, qi, ki: (0, 0)),
        ],
        out_specs=pl.BlockSpec((t, DIFF_W), lambda b, qi, ki: (b * nt + qi, 0)),
        scratch_shapes=[pltpu.VMEM((DIFF_HEADS, 2 * t, 1), F32), pltpu.VMEM((DIFF_HEADS, 2 * t, 1), F32),
                        pltpu.VMEM((DIFF_HEADS, 2 * t, LANES), F32)],
        out_shape=jax.ShapeDtypeStruct((nb * seq, DIFF_W), BF16),
        compiler_params=_params(("parallel", "parallel", "arbitrary")), name="diff_prompt",
    )(far, lam, dqn, dkb, dvb, bias_tiles, out_gain.reshape(1, LANES))


def _mla_decode_kernel(pt_ref, q_ref, knew_ref, wuv_ref, *rest, pages, nsteps):
    c_refs = rest[:pages]
    kr_refs = rest[pages:2 * pages]
    o_ref, m_sc, l_sc, acc_sc = rest[2 * pages:]
    step = pl.program_id(1)
    q = q_ref[0]

    @pl.when(step == 0)
    def _():
        k_new = knew_ref[0].astype(F32)
        m_sc[...] = jnp.sum(q.astype(F32) * k_new, axis=1, keepdims=True)
        l_sc[...] = jnp.ones_like(l_sc)
        acc_sc[...] = jnp.broadcast_to(k_new[:, :MLA_KV_LORA], acc_sc.shape)

    q_lat = q[:, :MLA_KV_LORA]
    q_rope = q[:, MLA_KV_LORA:MLA_KV_LORA + MLA_ROPE]
    cs = [c_refs[j][0, 0].astype(BF16) for j in range(pages)]
    s = jnp.concatenate(
        [_dot_nt(q_lat, cs[j]) + _dot_nt(q_rope, kr_refs[j][0, 0].astype(BF16)) for j in range(pages)],
        axis=1)
    m_prev = m_sc[...]
    m_new = jnp.maximum(m_prev, jnp.max(s, axis=1, keepdims=True))
    p = jnp.exp(s - m_new)
    alpha = jnp.exp(m_prev - m_new)
    l_sc[...] = alpha * l_sc[...] + jnp.sum(p, axis=1, keepdims=True)
    pb = p.astype(BF16)
    pv = _dot(pb[:, 0:LANES], cs[0])
    for j in range(1, pages):
        pv = pv + _dot(pb[:, j * LANES:(j + 1) * LANES], cs[j])
    acc_sc[...] = alpha * acc_sc[...] + pv
    m_sc[...] = m_new

    @pl.when(step == nsteps - 1)
    def _():
        o_lat = (acc_sc[...] / l_sc[...]).astype(BF16)
        for h in range(MLA_HEADS):
            o_ref[0, :, h * MLA_DV:(h + 1) * MLA_DV] = _dot(o_lat, wuv_ref[h])[h:h + 1]


def mla_decode(page_table, q_s, knew, w_uv, cache_c, cache_kr, layer, pages):
    n, n_pages = page_table.shape
    page = cache_c.shape[2]
    nsteps = n_pages // pages
    pt = page_table.reshape(-1)

    def page_spec(width, j):
        return pl.BlockSpec((1, 1, page, width),
                            lambda i, s, pt_ref, j=j: (layer, pt_ref[i * n_pages + s * pages + j], 0, 0))

    grid_spec = pltpu.PrefetchScalarGridSpec(
        num_scalar_prefetch=1, grid=(n, nsteps),
        in_specs=[pl.BlockSpec((1, MLA_HEADS, MLA_KW), lambda i, s, _: (i, 0, 0)),
                  pl.BlockSpec((1, 1, MLA_KW), lambda i, s, _: (i, 0, 0)),
                  pl.BlockSpec((MLA_HEADS, MLA_KV_LORA, MLA_DV), lambda i, s, _: (0, 0, 0))]
                 + [page_spec(MLA_KV_LORA, j) for j in range(pages)]
                 + [page_spec(MLA_ROPE, j) for j in range(pages)],
        out_specs=pl.BlockSpec((1, 1, MLA_W), lambda i, s, _: (i, 0, 0)),
        scratch_shapes=[pltpu.VMEM((MLA_HEADS, 1), F32), pltpu.VMEM((MLA_HEADS, 1), F32),
                        pltpu.VMEM((MLA_HEADS, MLA_KV_LORA), F32)])
    assert page == LANES
    return pl.pallas_call(
        functools.partial(_mla_decode_kernel, pages=pages, nsteps=nsteps),
        grid_spec=grid_spec,
        out_shape=jax.ShapeDtypeStruct((n, 1, MLA_W), F32),
        compiler_params=_params(("parallel", "arbitrary")), name="mla_decode",
    )(pt, q_s, knew, w_uv, *([cache_c] * pages), *([cache_kr] * pages))


def _diff_decode_kernel(pt_ref, lam_ref, q_ref, knew_ref, vnew_ref, bnew_ref, bias_ref, gain_ref,
                        *rest, pages, nsteps, out_scale):
    k_refs = rest[:pages]
    v_refs = rest[pages:2 * pages]
    o_ref, m_sc, l_sc, acc_sc = rest[2 * pages:]
    step = pl.program_id(1)
    nh = DIFF_HEADS
    qz = _two_map_rows(q_ref[0])

    @pl.when(step == 0)
    def _():
        b_new = bnew_ref[...]
        s_new = jnp.sum(qz.astype(F32) * knew_ref[0].astype(F32), axis=1, keepdims=True)
        m_sc[...] = s_new + jnp.concatenate([b_new, b_new], axis=0)
        l_sc[...] = jnp.ones_like(l_sc)
        acc_sc[...] = jnp.broadcast_to(vnew_ref[0].astype(F32), acc_sc.shape)

    s = jnp.concatenate([_dot_nt(qz, k_refs[j][0, 0].astype(BF16)) for j in range(pages)], axis=1)
    bias = bias_ref[...]
    s = s + jnp.concatenate([bias, bias], axis=0)
    m_prev = m_sc[...]
    m_new = jnp.maximum(m_prev, jnp.max(s, axis=1, keepdims=True))
    p = jnp.exp(s - m_new)
    alpha = jnp.exp(m_prev - m_new)
    l_sc[...] = alpha * l_sc[...] + jnp.sum(p, axis=1, keepdims=True)
    pb = p.astype(BF16)
    pv = _dot(pb[:, 0:LANES], v_refs[0][0, 0].astype(BF16))
    for j in range(1, pages):
        pv = pv + _dot(pb[:, j * LANES:(j + 1) * LANES], v_refs[j][0, 0].astype(BF16))
    acc_sc[...] = alpha * acc_sc[...] + pv
    m_sc[...] = m_new

    @pl.when(step == nsteps - 1)
    def _():
        o = acc_sc[...] / l_sc[...]
        o = o[:nh] - lam_ref[0] * o[nh:]
        o = o * lax.rsqrt(jnp.mean(o * o, axis=-1, keepdims=True) + NORM_EPS) * gain_ref[...]
        o_ref[0] = o * out_scale


def diff_decode(page_table, lam, q_s, knew, vnew, bias_new, bias_past, out_gain, out_scale,
                cache_k, cache_v, layer, pages):
    n, n_pages = page_table.shape
    page = cache_k.shape[2]
    nsteps = n_pages // pages
    pt = page_table.reshape(-1)

    def page_spec(j):
        return pl.BlockSpec((1, 1, page, LANES),
                            lambda i, s, pt_ref, j=j: (layer, pt_ref[i * n_pages + s * pages + j], 0, 0))

    grid_spec = pltpu.PrefetchScalarGridSpec(
        num_scalar_prefetch=1, grid=(n, nsteps),
        in_specs=[pl.BlockSpec(memory_space=pltpu.SMEM),
                  pl.BlockSpec((1, DIFF_HEADS, LANES), lambda i, s, *_: (i, 0, 0)),
                  pl.BlockSpec((1, 1, LANES), lambda i, s, *_: (i, 0, 0)),
                  pl.BlockSpec((1, 1, LANES), lambda i, s, *_: (i, 0, 0)),
                  pl.BlockSpec((DIFF_HEADS, 1), lambda i, s, *_: (0, 0)),
                  pl.BlockSpec((DIFF_HEADS, pages * page), lambda i, s, *_: (0, s)),
                  pl.BlockSpec((1, LANES), lambda i, s, *_: (0, 0))]
                 + [page_spec(j) for j in range(pages)] * 2,
        out_specs=pl.BlockSpec((1, DIFF_HEADS, LANES), lambda i, s, *_: (i, 0, 0)),
        scratch_shapes=[pltpu.VMEM((2 * DIFF_HEADS, 1), F32), pltpu.VMEM((2 * DIFF_HEADS, 1), F32),
                        pltpu.VMEM((2 * DIFF_HEADS, LANES), F32)])
    assert page == LANES
    return pl.pallas_call(
        functools.partial(_diff_decode_kernel, pages=pages, nsteps=nsteps, out_scale=out_scale),
        grid_spec=grid_spec,
        out_shape=jax.ShapeDtypeStruct((n, DIFF_HEADS, LANES), F32),
        compiler_params=_params(("parallel", "arbitrary")), name="diff_decode",
    )(pt, lam, q_s, knew, vnew, bias_new, bias_past, out_gain.reshape(1, LANES),
      *([cache_k] * pages), *([cache_v] * pages))


def _merge_kernel(h0_ref, h1_ref, h2_ref, w0_ref, w1_ref, w2_ref, g0_ref, g1_ref, g2_ref, o_ref):
    acc = jax.nn.sigmoid(g0_ref[...]) * _dot(h0_ref[...], w0_ref[...])
    acc = acc + jax.nn.sigmoid(g1_ref[...]) * _dot(h1_ref[...], w1_ref[...])
    acc = acc + jax.nn.sigmoid(g2_ref[...]) * _dot(h2_ref[...], w2_ref[...])
    o_ref[...] = acc.astype(o_ref.dtype)


def gated_merge(hs, ws, proj, tm, tn):
    m, kd = hs[0].shape
    d = ws[0].shape[1]
    nblk = d // tn
    h_spec = pl.BlockSpec((tm, kd), lambda i, j: (i, 0))
    w_spec = pl.BlockSpec((kd, tn), lambda i, j: (0, j))
    g_specs = [pl.BlockSpec((tm, tn), lambda i, j, b=b: (i, COL_GATE // tn + b * nblk + j)) for b in range(3)]
    return pl.pallas_call(
        _merge_kernel, grid=(m // tm, nblk),
        in_specs=[h_spec] * 3 + [w_spec] * 3 + g_specs,
        out_specs=pl.BlockSpec((tm, tn), lambda i, j: (i, j)),
        out_shape=jax.ShapeDtypeStruct((m, d), BF16),
        compiler_params=_params(("parallel", "parallel")), name="gated_merge",
    )(*hs, *ws, proj, proj, proj)


def _swiglu_up_kernel(x_ref, wg_ref, wu_ref, *rest, gated):
    x = x_ref[...]
    a = _dot(x, wg_ref[0])
    act = a * jax.nn.sigmoid(a) * _dot(x, wu_ref[0])
    if gated:
        gate_ref, o_ref = rest
        lane = lax.broadcasted_iota(jnp.int32, gate_ref.shape, 1)
        act = act * jnp.sum(jnp.where(lane == pl.program_id(1), gate_ref[...], 0.0), axis=1, keepdims=True)
    else:
        o_ref, = rest
    o_ref[...] = act.astype(o_ref.dtype)


def swiglu_up(x, wg, wu, tm, tn, gates=None):
    m, d = x.shape
    ne, _, f = wg.shape
    nf = f // tn
    in_specs = [pl.BlockSpec((tm, d), lambda i, e, j: (i, 0)),
                pl.BlockSpec((1, d, tn), lambda i, e, j: (e, 0, j)),
                pl.BlockSpec((1, d, tn), lambda i, e, j: (e, 0, j))]
    args = [x, wg, wu]
    if gates is not None:
        in_specs.append(pl.BlockSpec((tm, LANES), lambda i, e, j: (i, 0)))
        args.append(gates)
    return pl.pallas_call(
        functools.partial(_swiglu_up_kernel, gated=gates is not None),
        grid=(m // tm, ne, nf), in_specs=in_specs,
        out_specs=pl.BlockSpec((tm, tn), lambda i, e, j: (i, e * nf + j)),
        out_shape=jax.ShapeDtypeStruct((m, ne * f), BF16),
        compiler_params=_params(("parallel", "parallel", "parallel")), name="swiglu_up")(*args)


def _router_kernel(x_ref, g_ref, w_ref, o_ref):
    x = x_ref[...]
    h = x * lax.rsqrt(jnp.mean(x * x, axis=-1, keepdims=True) + NORM_EPS) * g_ref[...]
    logits = lax.dot_general(h, w_ref[...], (((1,), (0,)), ((), ())), precision=HIGHEST,
                             preferred_element_type=F32)
    lane = lax.broadcasted_iota(jnp.int32, logits.shape, 1)
    logits = jnp.where(lane < N_EXPERTS, logits, -jnp.inf)
    v1 = jnp.max(logits, axis=1, keepdims=True)
    i1 = jnp.min(jnp.where(logits == v1, lane, LANES), axis=1, keepdims=True)
    rest = jnp.where(lane == i1, -jnp.inf, logits)
    v2 = jnp.max(rest, axis=1, keepdims=True)
    i2 = jnp.min(jnp.where(rest == v2, lane, LANES), axis=1, keepdims=True)
    e2 = jnp.exp(v2 - v1)
    w1 = 1.0 / (1.0 + e2)
    o_ref[...] = jnp.where(lane == i1, w1, 0.0) + jnp.where(lane == i2, e2 * w1, 0.0)


def moe_gates(x, g, router_pad, tm):
    m, d = x.shape
    return pl.pallas_call(
        _router_kernel, grid=(m // tm,),
        in_specs=[pl.BlockSpec((tm, d), lambda i: (i, 0)), pl.BlockSpec((1, d), lambda i: (0, 0)),
                  pl.BlockSpec((d, LANES), lambda i: (0, 0))],
        out_specs=pl.BlockSpec((tm, LANES), lambda i: (i, 0)),
        out_shape=jax.ShapeDtypeStruct((m, LANES), F32),
        compiler_params=_params(("parallel",)), name="moe_gates")(x, g.reshape(1, d), router_pad)


def _rel_bucket(dist):
    n = jnp.maximum(dist, 0)
    max_exact = REL_BUCKETS // 2
    nf = jnp.maximum(n, max_exact).astype(F32)
    large = max_exact + (jnp.log(nf / max_exact) / math.log(REL_MAX_DIST / max_exact)
                         * (REL_BUCKETS - max_exact)).astype(jnp.int32)
    large = jnp.minimum(large, REL_BUCKETS - 1)
    return jnp.where(n < max_exact, n, large)


def _layer_weights(l, w_in, ml_gate_bias, mla_w_uq, mla_q_gain, mla_kr_gain, mla_w_uk, mla_w_uv,
                   diff_q_gain, diff_k_gain, diff_lambda):
    d = w_in.shape[1]
    offs = {}
    o = 0
    for name, width in (("mlq", ML_HEADS * ML_DK), ("mlk", ML_HEADS * ML_DK), ("mlv", ML_W), ("mlo", ML_W),
                        ("i", ML_HEADS), ("f", ML_HEADS), ("qa", MLA_Q_LORA), ("ckv", MLA_KV_LORA),
                        ("kr", MLA_ROPE), ("dq", DIFF_W), ("dk", 2 * DIFF_DH), ("dv", 2 * DIFF_DH),
                        ("gate", 3 * D_MODEL)):
        offs[name] = (o, o + width)
        o += width
    w = w_in[l]
    col = lambda name: w[:, offs[name][0]:offs[name][1]]
    zeros = lambda n: jnp.zeros((d, n), w.dtype)
    w_in_r = jnp.concatenate(
        [col("gate"), col("mlq"), col("mlk"), col("mlv"), col("mlo"), col("dq"), col("qa"), col("ckv"),
         col("dk"), col("dv"), col("kr"), zeros(LANES - MLA_ROPE), col("i"), col("f"),
         zeros(LANES - 2 * ML_HEADS)], axis=1).astype(BF16)
    assert w_in_r.shape[1] == PROJ_COLS

    uq = mla_w_uq[l].reshape(MLA_Q_LORA, MLA_HEADS, MLA_NOPE + MLA_ROPE)
    uq_rope = jnp.pad(uq[:, :, MLA_NOPE:], ((0, 0), (0, 0), (0, LANES - MLA_ROPE)))
    w_uq = jnp.concatenate([uq[:, :, :MLA_NOPE].reshape(MLA_Q_LORA, -1),
                            uq_rope.reshape(MLA_Q_LORA, -1)], axis=1).astype(BF16)
    pad64 = lambda v: jnp.pad(v, (0, LANES - MLA_ROPE)).reshape(1, LANES)
    lq1, lk1, lq2, lk2 = diff_lambda[l]
    lam_init = 0.8 - 0.6 * math.exp(-0.3 * l)
    lam = jnp.exp(jnp.sum(lq1 * lk1)) - jnp.exp(jnp.sum(lq2 * lk2)) + lam_init
    bias_row = jnp.pad(jnp.concatenate([ml_gate_bias[l, 0], ml_gate_bias[l, 1]]),
                       (0, LANES - 2 * ML_HEADS)).reshape(1, LANES)
    seg = jnp.arange(LANES) // DIFF_DH
    return dict(
        w_in=w_in_r, w_uq=w_uq, bias_row=bias_row, bias2=ml_gate_bias[l],
        gq_nope=mla_q_gain[l, :MLA_NOPE].reshape(1, LANES), gq_rope=pad64(mla_q_gain[l, MLA_NOPE:]),
        mla_kr_gain=pad64(mla_kr_gain[l]),
        w_uk_t=jnp.transpose(mla_w_uk[l], (1, 2, 0)).astype(BF16),
        w_uv=jnp.transpose(mla_w_uv[l], (1, 0, 2)).astype(BF16),
        diff_q_gain=jnp.tile(diff_q_gain[l], 2).reshape(1, LANES),
        diff_k_gain=jnp.tile(diff_k_gain[l], 2).reshape(1, LANES),
        seg_ones=(seg[:, None] == seg[None, :]).astype(F32),
        lam=lam.reshape(1).astype(F32), lam_init=lam_init)


def _token_mixer(l, x_all, dims, caches, states, page_table, rel_bias_table, wp, norm_mix_l,
                 ml_out_norm_l, mla_q_norm_l, mla_kv_norm_l, diff_out_norm_l, w_branch, w_out_l, tables):
    nb, seq, ns = dims
    mp = nb * seq
    cos, sin, bias_tiles_idx, bias_past_idx = tables
    cache_mla_c, cache_mla_kr, cache_diff_k, cache_diff_v = caches
    state_c, state_n, state_m = states

    m_all = x_all.shape[0]
    tm_big, tm_mid, tm_small = _tile(m_all, 1664, 640), _tile(m_all, 640), _tile(m_all, 320)
    h = rmsnorm_rows(x_all, norm_mix_l, tm_small)
    proj = matmul(h, wp["w_in"], tm_big, 512, D_MODEL)

    wp = dict(wp, mla_q_norm=mla_q_norm_l.reshape(1, -1), mla_kv_norm=mla_kv_norm_l.reshape(1, -1))
    ckv, kr, dk, kcat, dkb, dvb, qm, dqn = mixer_prep(proj, cos, sin, wp, PREP_TM)
    dv = proj[:, COL_DV:COL_DV + 2 * DIFF_DH]
    kr = kr[:, :MLA_ROPE]

    hml_p, pc, pn, pm = mlstm_prompt(proj, wp["bias_row"], ml_out_norm_l, nb, seq, ML_CHUNK)
    ps = proj[mp:]
    heads_last = lambda a: jnp.transpose(a.reshape(ns, ML_HEADS, ML_DK), (0, 2, 1))
    qt = heads_last(ps[:, COL_MLQ:COL_MLQ + ML_HEADS * ML_DK])
    kt = heads_last(ps[:, COL_MLK:COL_MLK + ML_HEADS * ML_DK])
    ig = ps[:, COL_IF:COL_IF + ML_HEADS].reshape(ns, 1, ML_HEADS)
    fg = ps[:, COL_IF + ML_HEADS:COL_IF + 2 * ML_HEADS].reshape(ns, 1, ML_HEADS)
    hml_s, sc, sn, sm = mlstm_sample(proj, mp, qt, kt, ig, fg, wp["bias2"], ml_out_norm_l,
                                     state_c, state_n, state_m, l, SAMPLE_G)
    h_ml = jnp.concatenate([hml_p, hml_s.astype(BF16)], axis=0)

    hmla_p = mla_prompt(qm, kcat, wp["w_uv"], nb, seq, ATT_T)
    q_s = jnp.transpose(qm[:, mp:], (1, 0, 2))
    hmla_s = mla_decode(page_table, q_s, kcat[mp:].reshape(ns, 1, MLA_KW), wp["w_uv"],
                        cache_mla_c, cache_mla_kr, l, DEC_PAGES)
    h_mla = jnp.concatenate([hmla_p, hmla_s.reshape(ns, MLA_W).astype(BF16)], axis=0)

    out_scale = 1.0 - wp["lam_init"]
    table = rel_bias_table.astype(F32)
    bias_tiles = jnp.transpose(table[bias_tiles_idx], (0, 3, 1, 2))
    far = table[REL_BUCKETS - 1]
    bias_past = jnp.transpose(table[bias_past_idx])
    hdiff_p = diff_prompt(dqn, dkb, dvb, bias_tiles, far, wp["lam"], diff_out_norm_l, out_scale,
                          nb, seq, ATT_T)
    hdiff_s = diff_decode(page_table, wp["lam"], jnp.transpose(dqn[:, mp:], (1, 0, 2)),
                          dkb[mp:].reshape(ns, 1, LANES), dvb[mp:].reshape(ns, 1, LANES),
                          bias_past[:, -1:], bias_past[:, :-1], diff_out_norm_l, out_scale,
                          cache_diff_k, cache_diff_v, l, DEC_PAGES)
    h_diff = jnp.concatenate([hdiff_p, hdiff_s.reshape(ns, DIFF_W).astype(BF16)], axis=0)

    merged = gated_merge([h_ml, h_mla, h_diff], w_branch, proj, tm_mid, 512)
    x_new = matmul(merged, w_out_l, tm_mid, 512, D_MODEL, res=x_all)

    split = lambda a: (a[:mp].reshape(nb, seq, -1), a[mp:].reshape(ns, 1, -1))
    ckv_p, ckv_s = split(ckv)
    kr_p, kr_s = split(kr)
    dk_p, dk_s = split(dk)
    dv_p, dv_s = split(dv)
    st_p = (ckv_p, kr_p, dk_p, dv_p, pc, pn.reshape(nb, ML_HEADS, ML_DK), pm[:, :, 0, 0])
    st_s = (ckv_s, kr_s, dk_s, dv_s, sc, sn, sm.reshape(ns, ML_HEADS))
    return x_new, st_p, st_s


def kernel(x_prompt, x_sample, cache_mla_c, cache_mla_kr, cache_diff_k, cache_diff_v, state_mlstm_c, state_mlstm_n, state_mlstm_m, page_table, rel_bias_table, norm_mix, norm_ffn, w_in, ml_gate_bias, ml_out_norm, mla_q_norm, mla_w_uq, mla_q_gain, mla_kv_norm, mla_kr_gain, mla_w_uk, mla_w_uv, diff_q_gain, diff_k_gain, diff_lambda, diff_out_norm, w_ml_o, w_mla_o, w_diff_o, w_out, ffn_w_gate, ffn_w_up, ffn_w_down, moe_router, moe_w_gate, moe_w_up, moe_w_down):
    nb, seq, d = x_prompt.shape
    ns, dec_seq, _ = x_sample.shape
    assert dec_seq == 1 and d == D_MODEL
    depth = w_in.shape[0]
    n_pages = page_table.shape[1]
    past = n_pages * cache_mla_c.shape[2]
    mp = nb * seq
    x_all = jnp.concatenate([x_prompt.reshape(mp, d), x_sample.reshape(ns, d)], axis=0)

    pos = jnp.concatenate([jnp.tile(jnp.arange(seq, dtype=jnp.int32), nb),
                           jnp.full((ns,), past, jnp.int32)])
    inv = ROPE_THETA ** (-jnp.arange(0, MLA_ROPE, 2, dtype=F32) / MLA_ROPE)
    ang = pos.astype(F32)[:, None] * inv[None, :]
    zpad = jnp.zeros((mp + ns, LANES - MLA_ROPE), F32)
    cos = jnp.concatenate([jnp.cos(ang), jnp.cos(ang), zpad], axis=1)
    sin = jnp.concatenate([-jnp.sin(ang), jnp.sin(ang), zpad], axis=1)
    t_i = jnp.arange(ATT_T, dtype=jnp.int32)
    delta = t_i[:, None] - t_i[None, :]
    bias_tiles_idx = jnp.stack([_rel_bucket(delta), _rel_bucket(delta + ATT_T)])
    bias_past_idx = _rel_bucket(past - jnp.arange(past + 1, dtype=jnp.int32))
    tables = (cos, sin, bias_tiles_idx, bias_past_idx)

    per_p, per_s = [], []
    for l in range(depth):
        wp = _layer_weights(l, w_in, ml_gate_bias, mla_w_uq, mla_q_gain, mla_kr_gain, mla_w_uk, mla_w_uv,
                            diff_q_gain, diff_k_gain, diff_lambda)
        w_branch = [w_ml_o[l].astype(BF16), w_mla_o[l].astype(BF16), w_diff_o[l].astype(BF16)]
        x_all, st_p, st_s = _token_mixer(
            l, x_all, (nb, seq, ns), (cache_mla_c, cache_mla_kr, cache_diff_k, cache_diff_v),
            (state_mlstm_c, state_mlstm_n, state_mlstm_m), page_table, rel_bias_table, wp, norm_mix[l],
            ml_out_norm[l], mla_q_norm[l], mla_kv_norm[l], diff_out_norm[l], w_branch,
            w_out[l].astype(BF16), tables)
        per_p.append(st_p)
        per_s.append(st_s)

        i = l // 2
        m_all = x_all.shape[0]
        tm_big, tm_mid, tm_small = _tile(m_all, 1664, 640), _tile(m_all, 640), _tile(m_all, 320)
        hf = rmsnorm_rows(x_all, norm_ffn[l], tm_small)
        if l % 2 == 0:
            act = swiglu_up(hf, ffn_w_gate[i:i + 1].astype(BF16), ffn_w_up[i:i + 1].astype(BF16), tm_big, 256)
            f = act.shape[1]
            x_all = matmul(act, ffn_w_down[i].astype(BF16), tm_mid, 512, f // 2, res=x_all)
        else:
            router_pad = jnp.pad(moe_router[i], ((0, 0), (0, LANES - N_EXPERTS)))
            gates = moe_gates(x_all, norm_ffn[l], router_pad, tm_small)
            act = swiglu_up(hf, moe_w_gate[i].astype(BF16), moe_w_up[i].astype(BF16), tm_big, 256, gates=gates)
            wd = moe_w_down[i].reshape(-1, d).astype(BF16)
            x_all = matmul(act, wd, tm_mid, 512, 4096, res=x_all)

    outs_p = [jnp.stack([st[j] for st in per_p]) for j in range(7)]
    outs_s = [jnp.stack([st[j] for st in per_s]) for j in range(7)]
    y_prompt = x_all[:mp].reshape(nb, seq, d)
    y_sample = x_all[mp:].reshape(ns, 1, d)
    return (y_prompt, y_sample, *outs_p, *outs_s)
```

```python
import functools
import math

import jax
import jax.numpy as jnp
from jax import lax
from jax.experimental import pallas as pl
from jax.experimental.pallas import tpu as pltpu

F32 = jnp.float32
BF16 = jnp.bfloat16
HIGHEST = lax.Precision.HIGHEST

LANES = 128
SUBLANES = 8
VMEM_LIMIT_BYTES = 56 * 1024 * 1024

D_MODEL = 4096
ML_HEADS, ML_DK, ML_DV = 8, 128, 256
MLA_HEADS, MLA_NOPE, MLA_ROPE, MLA_DV = 16, 128, 64, 128
MLA_Q_LORA, MLA_KV_LORA = 768, 256
ROPE_THETA = 10000.0
DIFF_HEADS, DIFF_DH = 16, 64
REL_BUCKETS, REL_MAX_DIST = 32, 128
N_EXPERTS, TOP_K = 8, 2
NORM_EPS = 1e-6
MLA_SCALE = (MLA_NOPE + MLA_ROPE) ** -0.5
DIFF_SCALE = DIFF_DH ** -0.5
ML_W = ML_HEADS * ML_DV
MLA_W = MLA_HEADS * MLA_DV
DIFF_W = DIFF_HEADS * 2 * DIFF_DH
MLA_KW = MLA_KV_LORA + 2 * MLA_ROPE

COL_GATE = 0
COL_MLQ = COL_GATE + 3 * D_MODEL
COL_MLK = COL_MLQ + ML_HEADS * ML_DK
COL_MLV = COL_MLK + ML_HEADS * ML_DK
COL_MLO = COL_MLV + ML_W
COL_DQ = COL_MLO + ML_W
COL_QA = COL_DQ + DIFF_W
COL_DK = COL_QA + MLA_Q_LORA + MLA_KV_LORA
COL_DV = COL_DK + 2 * DIFF_DH
COL_KR = COL_DV + 2 * DIFF_DH
COL_IF = COL_KR + LANES
PROJ_COLS = COL_IF + LANES

ML_CHUNK = 256
ATT_T = 256
DEC_PAGES = 16
PREP_TM = 320
SAMPLE_G = 8


def _tile(n, *candidates):
    return next(c for c in candidates if n % c == 0)


def _params(sem):
    return pltpu.CompilerParams(dimension_semantics=sem, vmem_limit_bytes=VMEM_LIMIT_BYTES)


def _dot(a, b):
    return jnp.dot(a, b, preferred_element_type=F32)


def _dot_nt(a, b, precision=None):
    return lax.dot_general(a, b, (((1,), (1,)), ((), ())), precision=precision,
                           preferred_element_type=F32)


def _rmsnorm_kernel(x_ref, g_ref, o_ref):
    x = x_ref[...]
    y = x * lax.rsqrt(jnp.mean(x * x, axis=-1, keepdims=True) + NORM_EPS)
    o_ref[...] = (y * g_ref[...]).astype(o_ref.dtype)


def rmsnorm_rows(x, g, tm, out_dtype=BF16):
    m, d = x.shape
    return pl.pallas_call(
        _rmsnorm_kernel, grid=(m // tm,),
        in_specs=[pl.BlockSpec((tm, d), lambda i: (i, 0)), pl.BlockSpec((1, d), lambda i: (0, 0))],
        out_specs=pl.BlockSpec((tm, d), lambda i: (i, 0)),
        out_shape=jax.ShapeDtypeStruct((m, d), out_dtype),
        compiler_params=_params(("parallel",)), name="rmsnorm_rows")(x, g.reshape(1, d))


def _matmul_kernel(*refs, nk, has_res):
    if has_res:
        x_ref, w_ref, r_ref, o_ref = refs[:4]
    else:
        x_ref, w_ref, o_ref = refs[:3]
        r_ref = None

    def partial_product():
        return _dot(x_ref[...], w_ref[0].astype(BF16))

    def finish(acc):
        if has_res:
            acc = acc + r_ref[...]
        o_ref[...] = acc.astype(o_ref.dtype)

    if nk == 1:
        finish(partial_product())
        return
    acc_ref = refs[-1]
    k = pl.program_id(2)

    @pl.when(k == 0)
    def _():
        acc_ref[...] = jnp.zeros_like(acc_ref)

    acc_ref[...] += partial_product()

    @pl.when(k == nk - 1)
    def _():
        finish(acc_ref[...])


def matmul(x, w, lead, tm, tn, tk, res=None, out_dtype=F32):
    m, kd = x.shape
    n = w.shape[2]
    nk = kd // tk
    in_specs = [pl.BlockSpec((tm, tk), lambda i, j, k: (i, k)),
                pl.BlockSpec((1, tk, tn), lambda i, j, k: (lead, k, j))]
    args = [x, w]
    if res is not None:
        in_specs.append(pl.BlockSpec((tm, tn), lambda i, j, k: (i, j)))
        args.append(res)
    return pl.pallas_call(
        functools.partial(_matmul_kernel, nk=nk, has_res=res is not None),
        grid=(m // tm, n // tn, nk), in_specs=in_specs,
        out_specs=pl.BlockSpec((tm, tn), lambda i, j, k: (i, j)),
        out_shape=jax.ShapeDtypeStruct((m, n), out_dtype),
        scratch_shapes=[pltpu.VMEM((tm, tn), F32)] if nk > 1 else [],
        compiler_params=_params(("parallel", "parallel", "arbitrary")), name="matmul")(*args)


def _seg_rmsnorm(x, bd, gain, width):
    ssq = lax.dot_general(x * x, bd, (((1,), (0,)), ((), ())), precision=HIGHEST,
                          preferred_element_type=F32)
    return x * lax.rsqrt(ssq * (1.0 / width) + NORM_EPS) * gain


def _rope_lo64(x, cos, sin):
    swapped = pltpu.roll(x, MLA_ROPE // 2, 1) + pltpu.roll(x, LANES - MLA_ROPE // 2, 1)
    return x * cos + swapped * sin


def _prep_kernel(qa_ref, dq_ref, dk_ref, dv_ref, kr_ref, cos_ref, sin_ref,
                 qn_ref, wuq_ref, gqn_ref, gqr_ref, gkv_ref, gkr_ref, wuk_ref,
                 gdq_ref, gdk_ref, bd_ref,
                 ckv_o, kr_o, dk_o, kcat_o, dkb_o, dvb_o, qm_o, dqn_o):
    cos = cos_ref[...]
    sin = sin_ref[...]
    lane = lax.broadcasted_iota(jnp.int32, cos.shape, 1)
    lo64 = lane < MLA_ROPE

    ckv = qa_ref[:, MLA_Q_LORA:]
    ckv = ckv * lax.rsqrt(jnp.mean(ckv * ckv, axis=-1, keepdims=True) + NORM_EPS) * gkv_ref[...]
    ckv_o[...] = ckv
    kr = jnp.where(lo64, kr_ref[...], 0.0)
    kr = kr * lax.rsqrt(jnp.sum(kr * kr, axis=-1, keepdims=True) * (1.0 / MLA_ROPE) + NORM_EPS)
    kr = _rope_lo64(kr * gkr_ref[...], cos, sin)
    kr_o[...] = kr
    kcat_o[:, :MLA_KV_LORA] = ckv.astype(BF16)
    kcat_o[:, MLA_KV_LORA:] = kr.astype(BF16)

    bd = bd_ref[...]
    dk = _seg_rmsnorm(dk_ref[...], bd, gdk_ref[...], DIFF_DH)
    dk_o[...] = dk
    dkb_o[...] = dk.astype(BF16)
    dvb_o[...] = dv_ref[...].astype(BF16)
    gdq = gdq_ref[...]
    for h in range(DIFF_HEADS):
        sl = slice(h * LANES, (h + 1) * LANES)
        dqn_o[h] = (_seg_rmsnorm(dq_ref[:, sl], bd, gdq, DIFF_DH) * DIFF_SCALE).astype(BF16)

    qa = qa_ref[:, :MLA_Q_LORA]
    qa = qa * lax.rsqrt(jnp.mean(qa * qa, axis=-1, keepdims=True) + NORM_EPS) * qn_ref[...]
    qf = _dot(qa.astype(BF16), wuq_ref[...])
    gqn = gqn_ref[...]
    gqr = gqr_ref[...]
    nope_w = MLA_HEADS * MLA_NOPE
    for h in range(MLA_HEADS):
        nope = qf[:, h * MLA_NOPE:(h + 1) * MLA_NOPE]
        rope = qf[:, nope_w + h * LANES:nope_w + (h + 1) * LANES]
        ssq = (jnp.sum(nope * nope, axis=-1, keepdims=True)
               + jnp.sum(rope * rope, axis=-1, keepdims=True))
        inv = lax.rsqrt(ssq * (1.0 / (MLA_NOPE + MLA_ROPE)) + NORM_EPS)
        rope = _rope_lo64(rope * inv * gqr, cos, sin)
        q_lat = _dot((nope * inv * gqn).astype(BF16), wuk_ref[h])
        qm_o[h, :, :MLA_KV_LORA] = (q_lat * MLA_SCALE).astype(BF16)
        qm_o[h, :, MLA_KV_LORA:] = (rope * MLA_SCALE).astype(BF16)


def mixer_prep(proj, cos, sin, wp, tm):
    m = proj.shape[0]
    row = lambda w, cb: pl.BlockSpec((tm, w), lambda i, cb=cb: (i, cb))
    full = lambda a: pl.BlockSpec(a.shape, lambda i, nd=a.ndim: (0,) * nd)
    qa_w = MLA_Q_LORA + MLA_KV_LORA
    consts = [wp["mla_q_norm"], wp["w_uq"], wp["gq_nope"], wp["gq_rope"], wp["mla_kv_norm"],
              wp["mla_kr_gain"], wp["w_uk_t"], wp["diff_q_gain"], wp["diff_k_gain"], wp["seg_ones"]]
    outs = [((m, MLA_KV_LORA), F32), ((m, LANES), F32), ((m, LANES), F32),
            ((m, MLA_KW), BF16), ((m, LANES), BF16), ((m, LANES), BF16),
            ((MLA_HEADS, m, MLA_KW), BF16), ((DIFF_HEADS, m, LANES), BF16)]
    out_specs = [pl.BlockSpec((tm, s[-1]), lambda i: (i, 0)) if len(s) == 2
                 else pl.BlockSpec((s[0], tm, s[-1]), lambda i: (0, i, 0)) for s, _ in outs]
    return pl.pallas_call(
        _prep_kernel, grid=(m // tm,),
        in_specs=[row(qa_w, COL_QA // qa_w), row(DIFF_W, COL_DQ // DIFF_W), row(LANES, COL_DK // LANES),
                  row(LANES, COL_DV // LANES), row(LANES, COL_KR // LANES),
                  pl.BlockSpec((tm, LANES), lambda i: (i, 0)), pl.BlockSpec((tm, LANES), lambda i: (i, 0))]
                 + [full(a) for a in consts],
        out_specs=out_specs,
        out_shape=[jax.ShapeDtypeStruct(s, dt) for s, dt in outs],
        compiler_params=_params(("parallel",)), name="mixer_prep",
    )(proj, proj, proj, proj, proj, cos, sin, *consts)


def _log_sigmoid(x):
    return jnp.minimum(x, 0.0) - jnp.log(1.0 + jnp.exp(-jnp.abs(x)))


def _mlstm_prompt_kernel(q_ref, k_ref, v_ref, o_ref, if_ref, bias_ref, gain_ref, tri_ref,
                         h_o, c_o, n_o, m_o, c_sc, n_sc, m_sc, *, nc, chunk):
    hd = pl.program_id(1)
    ci = pl.program_id(2)

    @pl.when(ci == 0)
    def _():
        c_sc[...] = jnp.zeros_like(c_sc)
        n_sc[...] = jnp.zeros_like(n_sc)
        m_sc[...] = jnp.zeros_like(m_sc)

    g2 = if_ref[...] + bias_ref[...]
    cum = lax.dot_general(tri_ref[...], _log_sigmoid(g2), (((1,), (0,)), ((), ())), precision=HIGHEST,
                          preferred_element_type=F32)
    lane = lax.broadcasted_iota(jnp.int32, g2.shape, 1)
    ig_col = jnp.sum(jnp.where(lane == hd, g2, 0.0), axis=1, keepdims=True)
    b_col = jnp.sum(jnp.where(lane == hd + ML_HEADS, cum, 0.0), axis=1, keepdims=True)
    lane8 = lax.broadcasted_iota(jnp.int32, (SUBLANES, LANES), 1)
    ig_row = _dot_nt((lane8 == hd).astype(F32), g2, precision=HIGHEST)[0:1]
    b_row = _dot_nt((lane8 == hd + ML_HEADS).astype(F32), cum, precision=HIGHEST)[0:1]
    b_last = b_row[:, chunk - 1:chunk]

    m_prev = m_sc[:, 0:1]
    r_i = lax.broadcasted_iota(jnp.int32, (chunk, chunk), 0)
    c_i = lax.broadcasted_iota(jnp.int32, (chunk, chunk), 1)
    d = jnp.where(c_i <= r_i, b_col - b_row + ig_row, -jnp.inf)
    inter = b_col + m_prev
    m_t = jnp.maximum(inter, jnp.max(d, axis=1, keepdims=True))
    q = q_ref[...]
    qb = q.astype(BF16)
    ks = k_ref[...] * (ML_DK ** -0.5)
    v = v_ref[...].astype(BF16)
    s = _dot_nt(qb, ks.astype(BF16)) * jnp.exp(d - m_t)
    w_inter = jnp.exp(inter - m_t)
    c_prev = c_sc[...]
    n_prev = n_sc[...]
    num = _dot(s.astype(BF16), v) + w_inter * _dot(qb, c_prev.astype(BF16))
    den = jnp.sum(s, axis=1, keepdims=True) + w_inter * jnp.sum(q * n_prev, axis=1, keepdims=True)
    hh = num / jnp.maximum(jnp.abs(den), jnp.exp(-m_t))
    hn = hh * lax.rsqrt(jnp.mean(hh * hh, axis=-1, keepdims=True) + NORM_EPS) * gain_ref[0]
    h_o[...] = (hn * jax.nn.sigmoid(o_ref[...])).astype(h_o.dtype)

    g_col = b_last - b_col + ig_col
    m_new = jnp.maximum(b_last + m_prev, jnp.max(g_col, axis=0, keepdims=True))
    kw = ks * jnp.exp(g_col - m_new)
    decay = jnp.exp(b_last + m_prev - m_new)
    c_new = decay * c_prev + _dot(jnp.transpose(kw).astype(BF16), v)
    n_new = decay * n_prev + jnp.sum(kw, axis=0, keepdims=True)
    c_sc[...] = c_new
    n_sc[...] = n_new
    m_sc[...] = jnp.broadcast_to(m_new, m_sc.shape)

    @pl.when(ci == nc - 1)
    def _():
        c_o[0, 0] = c_new
        n_o[0, 0] = n_new
        m_o[0, 0] = jnp.broadcast_to(m_new, (1, LANES))


def mlstm_prompt(proj, bias_row, out_gain, nb, seq, chunk):
    nc = seq // chunk
    tri = jnp.tril(jnp.ones((chunk, chunk), F32))
    rowblk = lambda b, c: b * nc + c
    return pl.pallas_call(
        functools.partial(_mlstm_prompt_kernel, nc=nc, chunk=chunk),
        grid=(nb, ML_HEADS, nc),
        in_specs=[
            pl.BlockSpec((chunk, ML_DK), lambda b, h, c: (rowblk(b, c), COL_MLQ // ML_DK + h)),
            pl.BlockSpec((chunk, ML_DK), lambda b, h, c: (rowblk(b, c), COL_MLK // ML_DK + h)),
            pl.BlockSpec((chunk, ML_DV), lambda b, h, c: (rowblk(b, c), COL_MLV // ML_DV + h)),
            pl.BlockSpec((chunk, ML_DV), lambda b, h, c: (rowblk(b, c), COL_MLO // ML_DV + h)),
            pl.BlockSpec((chunk, LANES), lambda b, h, c: (rowblk(b, c), COL_IF // LANES)),
            pl.BlockSpec((1, LANES), lambda b, h, c: (0, 0)),
            pl.BlockSpec((1, 1, ML_DV), lambda b, h, c: (h, 0, 0)),
            pl.BlockSpec((chunk, chunk), lambda b, h, c: (0, 0)),
        ],
        out_specs=[
            pl.BlockSpec((chunk, ML_DV), lambda b, h, c: (rowblk(b, c), h)),
            pl.BlockSpec((1, 1, ML_DK, ML_DV), lambda b, h, c: (b, h, 0, 0)),
            pl.BlockSpec((1, 1, 1, ML_DK), lambda b, h, c: (b, h, 0, 0)),
            pl.BlockSpec((1, 1, 1, LANES), lambda b, h, c: (b, h, 0, 0)),
        ],
        out_shape=[jax.ShapeDtypeStruct((nb * seq, ML_W), BF16),
                   jax.ShapeDtypeStruct((nb, ML_HEADS, ML_DK, ML_DV), F32),
                   jax.ShapeDtypeStruct((nb, ML_HEADS, 1, ML_DK), F32),
                   jax.ShapeDtypeStruct((nb, ML_HEADS, 1, LANES), F32)],
        scratch_shapes=[pltpu.VMEM((ML_DK, ML_DV), F32), pltpu.VMEM((1, ML_DK), F32),
                        pltpu.VMEM((1, LANES), F32)],
        compiler_params=_params(("parallel", "parallel", "arbitrary")), name="mlstm_prompt",
    )(proj, proj, proj, proj, proj, bias_row, out_gain.reshape(ML_HEADS, 1, ML_DV), tri)


def _mlstm_sample_kernel(q_ref, k_ref, v_ref, o_ref, qt_ref, kt_ref, ig_ref, fg_ref, bias_ref,
                         gain_ref, c_ref, n_ref, m_ref, h_o, c_o, n_o, m_o, *, g_seqs):
    lane8 = lax.broadcasted_iota(jnp.int32, (1, ML_HEADS), 1)
    for g in range(g_seqs):
        ig_all = ig_ref[g] + bias_ref[0:1, :]
        lf_all = _log_sigmoid(fg_ref[g] + bias_ref[1:2, :])
        m_all = m_ref[0, g]
        m_out = jnp.zeros((1, ML_HEADS), F32)
        for h in range(ML_HEADS):
            ig = ig_all[:, h:h + 1]
            inter = lf_all[:, h:h + 1] + m_all[:, h:h + 1]
            m_t = jnp.maximum(inter, ig)
            q = q_ref[g:g + 1, h * ML_DK:(h + 1) * ML_DK]
            k = k_ref[g:g + 1, h * ML_DK:(h + 1) * ML_DK] * (ML_DK ** -0.5)
            v = v_ref[g:g + 1, h * ML_DV:(h + 1) * ML_DV]
            q_col = qt_ref[g, :, h:h + 1]
            k_col = kt_ref[g, :, h:h + 1] * (ML_DK ** -0.5)
            c_prev = c_ref[0, g, h]
            n_prev = n_ref[0, g, h:h + 1, :]
            w_new = jnp.exp(ig - m_t)
            w_old = jnp.exp(inter - m_t)
            s = jnp.sum(q * k, axis=1, keepdims=True) * w_new
            num = s * v + w_old * jnp.sum(q_col * c_prev, axis=0, keepdims=True)
            den = s + w_old * jnp.sum(q * n_prev, axis=1, keepdims=True)
            hh = num / jnp.maximum(jnp.abs(den), jnp.exp(-m_t))
            hn = hh * lax.rsqrt(jnp.mean(hh * hh, axis=-1, keepdims=True) + NORM_EPS) * gain_ref[h]
            og = o_ref[g:g + 1, h * ML_DV:(h + 1) * ML_DV]
            h_o[g:g + 1, h * ML_DV:(h + 1) * ML_DV] = hn * jax.nn.sigmoid(og)
            c_o[g, h] = w_old * c_prev + (w_new * k_col) * v
            n_o[g, h:h + 1, :] = w_old * n_prev + w_new * k
            m_out = jnp.where(lane8 == h, m_t, m_out)
        m_o[g] = m_out


def mlstm_sample(proj, row0, qt, kt, ig, fg, bias2, out_gain, state_c, state_n, state_m, layer, g_seqs):
    n = state_c.shape[1]
    rb = row0 // g_seqs
    return pl.pallas_call(
        functools.partial(_mlstm_sample_kernel, g_seqs=g_seqs),
        grid=(n // g_seqs,),
        in_specs=[
            pl.BlockSpec((g_seqs, ML_HEADS * ML_DK), lambda i: (rb + i, COL_MLQ // (ML_HEADS * ML_DK))),
            pl.BlockSpec((g_seqs, ML_HEADS * ML_DK), lambda i: (rb + i, COL_MLK // (ML_HEADS * ML_DK))),
            pl.BlockSpec((g_seqs, ML_W), lambda i: (rb + i, COL_MLV // ML_W)),
            pl.BlockSpec((g_seqs, ML_W), lambda i: (rb + i, COL_MLO // ML_W)),
            pl.BlockSpec((g_seqs, ML_DK, ML_HEADS), lambda i: (i, 0, 0)),
            pl.BlockSpec((g_seqs, ML_DK, ML_HEADS), lambda i: (i, 0, 0)),
            pl.BlockSpec((g_seqs, 1, ML_HEADS), lambda i: (i, 0, 0)),
            pl.BlockSpec((g_seqs, 1, ML_HEADS), lambda i: (i, 0, 0)),
            pl.BlockSpec((2, ML_HEADS), lambda i: (0, 0)),
            pl.BlockSpec((ML_HEADS, 1, ML_DV), lambda i: (0, 0, 0)),
            pl.BlockSpec((1, g_seqs, ML_HEADS, ML_DK, ML_DV), lambda i: (layer, i, 0, 0, 0)),
            pl.BlockSpec((1, g_seqs, ML_HEADS, ML_DK), lambda i: (layer, i, 0, 0)),
            pl.BlockSpec((1, g_seqs, 1, ML_HEADS), lambda i: (layer, i, 0, 0)),
        ],
        out_specs=[
            pl.BlockSpec((g_seqs, ML_W), lambda i: (i, 0)),
            pl.BlockSpec((g_seqs, ML_HEADS, ML_DK, ML_DV), lambda i: (i, 0, 0, 0)),
            pl.BlockSpec((g_seqs, ML_HEADS, ML_DK), lambda i: (i, 0, 0)),
            pl.BlockSpec((g_seqs, 1, ML_HEADS), lambda i: (i, 0, 0)),
        ],
        out_shape=[jax.ShapeDtypeStruct((n, ML_W), F32),
                   jax.ShapeDtypeStruct((n, ML_HEADS, ML_DK, ML_DV), F32),
                   jax.ShapeDtypeStruct((n, ML_HEADS, ML_DK), F32),
                   jax.ShapeDtypeStruct((n, 1, ML_HEADS), F32)],
        compiler_params=_params(("parallel",)), name="mlstm_sample",
    )(proj, proj, proj, proj, qt, kt, ig, fg, bias2, out_gain.reshape(ML_HEADS, 1, ML_DV),
      state_c, state_n, state_m.reshape(state_m.shape[0], n, 1, ML_HEADS))


ATT_GROUP = 4


def _softmax_group_t(s_ts, v_t, m_sc, l_sc, acc_sc, g):
    n = len(s_ts)
    m_prev = [m_sc[g, j] for j in range(n)]
    l_prev = [l_sc[g, j] for j in range(n)]
    acc_prev = [acc_sc[g, j] for j in range(n)]
    m_new = [jnp.maximum(m_prev[j], jnp.max(s_ts[j], axis=0, keepdims=True)) for j in range(n)]
    p = [jnp.exp(s_ts[j] - m_new[j]) for j in range(n)]
    alpha = [jnp.exp(m_prev[j] - m_new[j]) for j in range(n)]
    pv = [_dot(v_t, p[j].astype(BF16)) for j in range(n)]
    for j in range(n):
        l_sc[g, j] = alpha[j] * l_prev[j] + jnp.sum(p[j], axis=0, keepdims=True)
        acc_sc[g, j] = alpha[j] * acc_prev[j] + pv[j]
        m_sc[g, j] = m_new[j]


def _key_visible(t, qi, ki):
    k_pos = lax.broadcasted_iota(jnp.int32, (t, t), 0) + ki * t
    q_pos = lax.broadcasted_iota(jnp.int32, (t, t), 1) + qi * t
    return k_pos <= q_pos


def _mla_prompt_kernel(q_ref, k_ref, wuvt_ref, o_ref, m_sc, l_sc, acc_sc, *, t):
    qi = pl.program_id(1)
    ki = pl.program_id(2)

    @pl.when(ki == 0)
    def _():
        m_sc[...] = jnp.full_like(m_sc, -jnp.inf)
        l_sc[...] = jnp.zeros_like(l_sc)
        acc_sc[...] = jnp.zeros_like(acc_sc)

    def sweep(masked):
        k = k_ref[...]
        c_t = jnp.transpose(k[:, :MLA_KV_LORA].astype(F32)).astype(BF16)
        visible = _key_visible(t, qi, ki) if masked else None

        def heads(g, carry):
            s_ts = [_dot_nt(k, q_ref[g * ATT_GROUP + j]) for j in range(ATT_GROUP)]
            if masked:
                s_ts = [jnp.where(visible, s_t, -jnp.inf) for s_t in s_ts]
            _softmax_group_t(s_ts, c_t, m_sc, l_sc, acc_sc, g)
            return carry

        lax.fori_loop(0, MLA_HEADS // ATT_GROUP, heads, 0)

    pl.when(ki < qi)(lambda: sweep(False))

    @pl.when(ki == qi)
    def _():
        sweep(True)
        for h in range(MLA_HEADS):
            g, j = divmod(h, ATT_GROUP)
            o_lat = (acc_sc[g, j] * (1.0 / l_sc[g, j])).astype(BF16)
            out_t = _dot(wuvt_ref[h], o_lat)
            o_ref[:, h * MLA_DV:(h + 1) * MLA_DV] = jnp.transpose(out_t).astype(o_ref.dtype)


def mla_prompt(qm, kcat, w_uv_t, nb, seq, t):
    nt = seq // t
    return pl.pallas_call(
        functools.partial(_mla_prompt_kernel, t=t),
        grid=(nb, nt, nt),
        in_specs=[
            pl.BlockSpec((MLA_HEADS, t, MLA_KW), lambda b, qi, ki: (0, b * nt + qi, 0)),
            pl.BlockSpec((t, MLA_KW), lambda b, qi, ki: (b * nt + jnp.minimum(ki, qi), 0)),
            pl.BlockSpec((MLA_HEADS, MLA_DV, MLA_KV_LORA), lambda b, qi, ki: (0, 0, 0)),
        ],
        out_specs=pl.BlockSpec((t, MLA_W), lambda b, qi, ki: (b * nt + qi, 0)),
        out_shape=jax.ShapeDtypeStruct((nb * seq, MLA_W), BF16),
        scratch_shapes=[pltpu.VMEM((MLA_HEADS // ATT_GROUP, ATT_GROUP, 1, t), F32),
                        pltpu.VMEM((MLA_HEADS // ATT_GROUP, ATT_GROUP, 1, t), F32),
                        pltpu.VMEM((MLA_HEADS // ATT_GROUP, ATT_GROUP, MLA_KV_LORA, t), F32)],
        compiler_params=_params(("parallel", "parallel", "arbitrary")), name="mla_prompt",
    )(qm, kcat, w_uv_t)


def _two_map_rows(q):
    lane = lax.broadcasted_iota(jnp.int32, q.shape, 1)
    zero = jnp.zeros_like(q)
    return jnp.concatenate([jnp.where(lane < DIFF_DH, q, zero), jnp.where(lane >= DIFF_DH, q, zero)], axis=0)


def _diff_prompt_kernel(far_ref, lam_ref, q_ref, k_ref, v_ref, bias_ref, gain_ref, o_ref,
                        m_sc, l_sc, acc_sc, *, t, out_scale):
    qi = pl.program_id(1)
    ki = pl.program_id(2)

    @pl.when(ki == 0)
    def _():
        m_sc[...] = jnp.full_like(m_sc, -jnp.inf)
        l_sc[...] = jnp.zeros_like(l_sc)
        acc_sc[...] = jnp.zeros_like(acc_sc)

    def sweep(kind):
        k = k_ref[...]
        lane = lax.broadcasted_iota(jnp.int32, k.shape, 1)
        zero = jnp.zeros_like(k)
        k_maps = (jnp.where(lane < DIFF_DH, k, zero), jnp.where(lane >= DIFF_DH, k, zero))
        v_t = jnp.transpose(v_ref[...].astype(F32)).astype(BF16)
        visible = _key_visible(t, qi, ki) if kind == 0 else None

        def heads(g, carry):
            s_ts = []
            for hh in range(hpg):
                h = g * hpg + hh
                q = q_ref[h]
                bias_t = far_ref[h] if kind == 2 else bias_ref[kind, h]
                for j in range(2):
                    s_t = _dot_nt(k_maps[j], q) + bias_t
                    s_ts.append(jnp.where(visible, s_t, -jnp.inf) if kind == 0 else s_t)
            _softmax_group_t(s_ts, v_t, m_sc, l_sc, acc_sc, g)
            return carry

        lax.fori_loop(0, DIFF_HEADS // hpg, heads, 0)

    hpg = ATT_GROUP // 2
    pl.when(ki == qi - 1)(lambda: sweep(1))
    pl.when(ki < qi - 1)(lambda: sweep(2))

    @pl.when(ki == qi)
    def _():
        sweep(0)
        lam = lam_ref[0]
        gain = gain_ref[...] * out_scale
        for h in range(DIFF_HEADS):
            g, j = divmod(2 * h, ATT_GROUP)
            o = (acc_sc[g, j] * (1.0 / l_sc[g, j])
                 - lam * (acc_sc[g, j + 1] * (1.0 / l_sc[g, j + 1])))
            o = o * lax.rsqrt(jnp.mean(o * o, axis=0, keepdims=True) + NORM_EPS) * gain
            o_ref[:, h * LANES:(h + 1) * LANES] = jnp.transpose(o).astype(o_ref.dtype)


def diff_prompt(dqn, dkb, dvb, bias_tiles, far, lam, out_gain, out_scale, nb, seq, t):
    nt = seq // t
    smem = pl.BlockSpec(memory_space=pltpu.SMEM)
    return pl.pallas_call(
        functools.partial(_diff_prompt_kernel, t=t, out_scale=out_scale),
        grid=(nb, nt, nt),
        in_specs=[
            smem, smem,
            pl.BlockSpec((DIFF_HEADS, t, LANES), lambda b, qi, ki: (0, b * nt + qi, 0)),
            pl.BlockSpec((t, LANES), lambda b, qi, ki: (b * nt + jnp.minimum(ki, qi), 0)),
            pl.BlockSpec((t, LANES), lambda b, qi, ki: (b * nt + jnp.minimum(ki, qi), 0)),
            pl.BlockSpec((2, DIFF_HEADS, t, t), lambda b, qi, ki: (0, 0, 0, 0)),
            pl.BlockSpec((LANES, 1), lambda b, qi, ki: (0, 0)),
        ],
        out_specs=pl.BlockSpec((t, DIFF_W), lambda b, qi, ki: (b * nt + qi, 0)),
        scratch_shapes=[pltpu.VMEM((2 * DIFF_HEADS // ATT_GROUP, ATT_GROUP, 1, t), F32),
                        pltpu.VMEM((2 * DIFF_HEADS // ATT_GROUP, ATT_GROUP, 1, t), F32),
                        pltpu.VMEM((2 * DIFF_HEADS // ATT_GROUP, ATT_GROUP, LANES, t), F32)],
        out_shape=jax.ShapeDtypeStruct((nb * seq, DIFF_W), BF16),
        compiler_params=_params(("parallel", "parallel", "arbitrary")), name="diff_prompt",
    )(far, lam, dqn, dkb, dvb, bias_tiles, out_gain.reshape(LANES, 1))


def _mla_decode_kernel(pt_ref, q_ref, knew_ref, wuv_ref, *rest, pages, nsteps):
    c_refs = rest[:pages]
    krt_refs = rest[pages:2 * pages]
    o_ref, m_sc, l_sc, acc_sc, c_buf, krt_buf = rest[2 * pages:]
    step = pl.program_id(1)
    q = q_ref[0]

    @pl.when(step == 0)
    def _():
        k_new = knew_ref[0].astype(F32)
        m_sc[...] = jnp.sum(q.astype(F32) * k_new, axis=1, keepdims=True)
        l_sc[...] = jnp.ones_like(l_sc)
        acc_sc[...] = jnp.broadcast_to(k_new[:, :MLA_KV_LORA], acc_sc.shape)

    for j in range(pages):
        c_buf[j * LANES:(j + 1) * LANES, :] = c_refs[j][0, 0].astype(BF16)
        krt_buf[:, j * LANES:(j + 1) * LANES] = krt_refs[j][0, 0].astype(BF16)
    c_all = c_buf[...]
    s = (_dot_nt(q[:, :MLA_KV_LORA], c_all)
         + _dot(q[:, MLA_KV_LORA:MLA_KV_LORA + MLA_ROPE], krt_buf[...]))
    m_prev = m_sc[...]
    m_new = jnp.maximum(m_prev, jnp.max(s, axis=1, keepdims=True))
    p = jnp.exp(s - m_new)
    alpha = jnp.exp(m_prev - m_new)
    l_sc[...] = alpha * l_sc[...] + jnp.sum(p, axis=1, keepdims=True)
    acc_sc[...] = alpha * acc_sc[...] + _dot(p.astype(BF16), c_all)
    m_sc[...] = m_new

    @pl.when(step == nsteps - 1)
    def _():
        o_lat = (acc_sc[...] / l_sc[...]).astype(BF16)
        for h in range(MLA_HEADS):
            o_ref[0, :, h * MLA_DV:(h + 1) * MLA_DV] = _dot(o_lat, wuv_ref[h])[h:h + 1]


def mla_decode(page_table, q_s, knew, w_uv, cache_c, cache_kr_t, layer, pages):
    n, n_pages = page_table.shape
    page = cache_c.shape[2]
    nsteps = n_pages // pages
    pt = page_table.reshape(-1)

    def page_spec(rows, width, j):
        return pl.BlockSpec((1, 1, rows, width),
                            lambda i, s, pt_ref, j=j: (layer, pt_ref[i * n_pages + s * pages + j], 0, 0))

    grid_spec = pltpu.PrefetchScalarGridSpec(
        num_scalar_prefetch=1, grid=(n, nsteps),
        in_specs=[pl.BlockSpec((1, MLA_HEADS, MLA_KW), lambda i, s, _: (i, 0, 0)),
                  pl.BlockSpec((1, 1, MLA_KW), lambda i, s, _: (i, 0, 0)),
                  pl.BlockSpec((MLA_HEADS, MLA_KV_LORA, MLA_DV), lambda i, s, _: (0, 0, 0))]
                 + [page_spec(page, MLA_KV_LORA, j) for j in range(pages)]
                 + [page_spec(MLA_ROPE, page, j) for j in range(pages)],
        out_specs=pl.BlockSpec((1, 1, MLA_W), lambda i, s, _: (i, 0, 0)),
        scratch_shapes=[pltpu.VMEM((MLA_HEADS, 1), F32), pltpu.VMEM((MLA_HEADS, 1), F32),
                        pltpu.VMEM((MLA_HEADS, MLA_KV_LORA), F32),
                        pltpu.VMEM((pages * page, MLA_KV_LORA), BF16),
                        pltpu.VMEM((MLA_ROPE, pages * page), BF16)])
    assert page == LANES
    return pl.pallas_call(
        functools.partial(_mla_decode_kernel, pages=pages, nsteps=nsteps),
        grid_spec=grid_spec,
        out_shape=jax.ShapeDtypeStruct((n, 1, MLA_W), F32),
        compiler_params=_params(("parallel", "arbitrary")), name="mla_decode",
    )(pt, q_s, knew, w_uv, *([cache_c] * pages), *([cache_kr_t] * pages))


def _diff_decode_kernel(pt_ref, lam_ref, q_ref, knew_ref, vnew_ref, bnew_ref, bias_ref, gain_ref,
                        *rest, pages, nsteps, out_scale):
    k_refs = rest[:pages]
    v_refs = rest[pages:2 * pages]
    o_ref, m_sc, l_sc, acc_sc, k_buf, v_buf = rest[2 * pages:]
    step = pl.program_id(1)
    nh = DIFF_HEADS
    qz = _two_map_rows(q_ref[0])

    @pl.when(step == 0)
    def _():
        b_new = bnew_ref[...]
        s_new = jnp.sum(qz.astype(F32) * knew_ref[0].astype(F32), axis=1, keepdims=True)
        m_sc[...] = s_new + jnp.concatenate([b_new, b_new], axis=0)
        l_sc[...] = jnp.ones_like(l_sc)
        acc_sc[...] = jnp.broadcast_to(vnew_ref[0].astype(F32), acc_sc.shape)

    for j in range(pages):
        k_buf[j * LANES:(j + 1) * LANES, :] = k_refs[j][0, 0].astype(BF16)
        v_buf[j * LANES:(j + 1) * LANES, :] = v_refs[j][0, 0].astype(BF16)
    bias = bias_ref[...]
    s = _dot_nt(qz, k_buf[...]) + jnp.concatenate([bias, bias], axis=0)
    m_prev = m_sc[...]
    m_new = jnp.maximum(m_prev, jnp.max(s, axis=1, keepdims=True))
    p = jnp.exp(s - m_new)
    alpha = jnp.exp(m_prev - m_new)
    l_sc[...] = alpha * l_sc[...] + jnp.sum(p, axis=1, keepdims=True)
    acc_sc[...] = alpha * acc_sc[...] + _dot(p.astype(BF16), v_buf[...])
    m_sc[...] = m_new

    @pl.when(step == nsteps - 1)
    def _():
        o = acc_sc[...] / l_sc[...]
        o = o[:nh] - lam_ref[0] * o[nh:]
        o = o * lax.rsqrt(jnp.mean(o * o, axis=-1, keepdims=True) + NORM_EPS) * gain_ref[...]
        o_ref[0] = o * out_scale


def diff_decode(page_table, lam, q_s, knew, vnew, bias_new, bias_past, out_gain, out_scale,
                cache_k, cache_v, layer, pages):
    n, n_pages = page_table.shape
    page = cache_k.shape[2]
    nsteps = n_pages // pages
    pt = page_table.reshape(-1)

    def page_spec(j):
        return pl.BlockSpec((1, 1, page, LANES),
                            lambda i, s, pt_ref, j=j: (layer, pt_ref[i * n_pages + s * pages + j], 0, 0))

    grid_spec = pltpu.PrefetchScalarGridSpec(
        num_scalar_prefetch=1, grid=(n, nsteps),
        in_specs=[pl.BlockSpec(memory_space=pltpu.SMEM),
                  pl.BlockSpec((1, DIFF_HEADS, LANES), lambda i, s, *_: (i, 0, 0)),
                  pl.BlockSpec((1, 1, LANES), lambda i, s, *_: (i, 0, 0)),
                  pl.BlockSpec((1, 1, LANES), lambda i, s, *_: (i, 0, 0)),
                  pl.BlockSpec((DIFF_HEADS, 1), lambda i, s, *_: (0, 0)),
                  pl.BlockSpec((DIFF_HEADS, pages * page), lambda i, s, *_: (0, s)),
                  pl.BlockSpec((1, LANES), lambda i, s, *_: (0, 0))]
                 + [page_spec(j) for j in range(pages)] * 2,
        out_specs=pl.BlockSpec((1, DIFF_HEADS, LANES), lambda i, s, *_: (i, 0, 0)),
        scratch_shapes=[pltpu.VMEM((2 * DIFF_HEADS, 1), F32), pltpu.VMEM((2 * DIFF_HEADS, 1), F32),
                        pltpu.VMEM((2 * DIFF_HEADS, LANES), F32),
                        pltpu.VMEM((pages * page, LANES), BF16), pltpu.VMEM((pages * page, LANES), BF16)])
    assert page == LANES
    return pl.pallas_call(
        functools.partial(_diff_decode_kernel, pages=pages, nsteps=nsteps, out_scale=out_scale),
        grid_spec=grid_spec,
        out_shape=jax.ShapeDtypeStruct((n, DIFF_HEADS, LANES), F32),
        compiler_params=_params(("parallel", "arbitrary")), name="diff_decode",
    )(pt, lam, q_s, knew, vnew, bias_new, bias_past, out_gain.reshape(1, LANES),
      *([cache_k] * pages), *([cache_v] * pages))


def _merge_kernel(h0_ref, h1_ref, h2_ref, w0_ref, w1_ref, w2_ref, g0_ref, g1_ref, g2_ref, o_ref):
    acc = jax.nn.sigmoid(g0_ref[...]) * _dot(h0_ref[...], w0_ref[0].astype(BF16))
    acc = acc + jax.nn.sigmoid(g1_ref[...]) * _dot(h1_ref[...], w1_ref[0].astype(BF16))
    acc = acc + jax.nn.sigmoid(g2_ref[...]) * _dot(h2_ref[...], w2_ref[0].astype(BF16))
    o_ref[...] = acc.astype(o_ref.dtype)


def gated_merge(hs, ws, lead, proj, tm, tn):
    m, kd = hs[0].shape
    d = ws[0].shape[2]
    nblk = d // tn
    h_spec = pl.BlockSpec((tm, kd), lambda i, j: (i, 0))
    w_spec = pl.BlockSpec((1, kd, tn), lambda i, j: (lead, 0, j))
    g_specs = [pl.BlockSpec((tm, tn), lambda i, j, b=b: (i, COL_GATE // tn + b * nblk + j)) for b in range(3)]
    return pl.pallas_call(
        _merge_kernel, grid=(m // tm, nblk),
        in_specs=[h_spec] * 3 + [w_spec] * 3 + g_specs,
        out_specs=pl.BlockSpec((tm, tn), lambda i, j: (i, j)),
        out_shape=jax.ShapeDtypeStruct((m, d), BF16),
        compiler_params=_params(("parallel", "parallel")), name="gated_merge",
    )(*hs, *ws, proj, proj, proj)


def _swiglu_up_kernel(x_ref, wg_ref, wu_ref, *rest, gated):
    x = x_ref[...]
    a = _dot(x, wg_ref[0].astype(BF16))
    act = a * jax.nn.sigmoid(a) * _dot(x, wu_ref[0].astype(BF16))
    if gated:
        gate_ref, o_ref = rest
        lane = lax.broadcasted_iota(jnp.int32, gate_ref.shape, 1)
        act = act * jnp.sum(jnp.where(lane == pl.program_id(1), gate_ref[...], 0.0), axis=1, keepdims=True)
    else:
        o_ref, = rest
    o_ref[...] = act.astype(o_ref.dtype)


def swiglu_up(x, wg, wu, tm, tn, gates=None):
    m, d = x.shape
    ne, _, f = wg.shape
    nf = f // tn
    in_specs = [pl.BlockSpec((tm, d), lambda i, e, j: (i, 0)),
                pl.BlockSpec((1, d, tn), lambda i, e, j: (e, 0, j)),
                pl.BlockSpec((1, d, tn), lambda i, e, j: (e, 0, j))]
    args = [x, wg, wu]
    if gates is not None:
        in_specs.append(pl.BlockSpec((tm, LANES), lambda i, e, j: (i, 0)))
        args.append(gates)
    return pl.pallas_call(
        functools.partial(_swiglu_up_kernel, gated=gates is not None),
        grid=(m // tm, ne, nf), in_specs=in_specs,
        out_specs=pl.BlockSpec((tm, tn), lambda i, e, j: (i, e * nf + j)),
        out_shape=jax.ShapeDtypeStruct((m, ne * f), BF16),
        compiler_params=_params(("parallel", "parallel", "parallel")), name="swiglu_up")(*args)


def _router_kernel(x_ref, g_ref, w_ref, o_ref):
    x = x_ref[...]
    h = x * lax.rsqrt(jnp.mean(x * x, axis=-1, keepdims=True) + NORM_EPS) * g_ref[...]
    logits = lax.dot_general(h, w_ref[...], (((1,), (0,)), ((), ())), precision=HIGHEST,
                             preferred_element_type=F32)
    lane = lax.broadcasted_iota(jnp.int32, logits.shape, 1)
    logits = jnp.where(lane < N_EXPERTS, logits, -jnp.inf)
    v1 = jnp.max(logits, axis=1, keepdims=True)
    i1 = jnp.min(jnp.where(logits == v1, lane, LANES), axis=1, keepdims=True)
    rest = jnp.where(lane == i1, -jnp.inf, logits)
    v2 = jnp.max(rest, axis=1, keepdims=True)
    i2 = jnp.min(jnp.where(rest == v2, lane, LANES), axis=1, keepdims=True)
    e2 = jnp.exp(v2 - v1)
    w1 = 1.0 / (1.0 + e2)
    o_ref[...] = jnp.where(lane == i1, w1, 0.0) + jnp.where(lane == i2, e2 * w1, 0.0)


def moe_gates(x, g, router_pad, tm):
    m, d = x.shape
    return pl.pallas_call(
        _router_kernel, grid=(m // tm,),
        in_specs=[pl.BlockSpec((tm, d), lambda i: (i, 0)), pl.BlockSpec((1, d), lambda i: (0, 0)),
                  pl.BlockSpec((d, LANES), lambda i: (0, 0))],
        out_specs=pl.BlockSpec((tm, LANES), lambda i: (i, 0)),
        out_shape=jax.ShapeDtypeStruct((m, LANES), F32),
        compiler_params=_params(("parallel",)), name="moe_gates")(x, g.reshape(1, d), router_pad)


def _rel_bucket(dist):
    n = jnp.maximum(dist, 0)
    max_exact = REL_BUCKETS // 2
    nf = jnp.maximum(n, max_exact).astype(F32)
    large = max_exact + (jnp.log(nf / max_exact) / math.log(REL_MAX_DIST / max_exact)
                         * (REL_BUCKETS - max_exact)).astype(jnp.int32)
    large = jnp.minimum(large, REL_BUCKETS - 1)
    return jnp.where(n < max_exact, n, large)


def _layer_weights(l, w_in, ml_gate_bias, mla_w_uq, mla_q_gain, mla_kr_gain, mla_w_uk, mla_w_uv,
                   diff_q_gain, diff_k_gain, diff_lambda):
    d = w_in.shape[1]
    offs = {}
    o = 0
    for name, width in (("mlq", ML_HEADS * ML_DK), ("mlk", ML_HEADS * ML_DK), ("mlv", ML_W), ("mlo", ML_W),
                        ("i", ML_HEADS), ("f", ML_HEADS), ("qa", MLA_Q_LORA), ("ckv", MLA_KV_LORA),
                        ("kr", MLA_ROPE), ("dq", DIFF_W), ("dk", 2 * DIFF_DH), ("dv", 2 * DIFF_DH),
                        ("gate", 3 * D_MODEL)):
        offs[name] = (o, o + width)
        o += width
    w = w_in[l]
    col = lambda name: w[:, offs[name][0]:offs[name][1]]
    zeros = lambda n: jnp.zeros((d, n), w.dtype)
    w_in_r = jnp.concatenate(
        [col("gate"), col("mlq"), col("mlk"), col("mlv"), col("mlo"), col("dq"), col("qa"), col("ckv"),
         col("dk"), col("dv"), col("kr"), zeros(LANES - MLA_ROPE), col("i"), col("f"),
         zeros(LANES - 2 * ML_HEADS)], axis=1).astype(BF16)
    assert w_in_r.shape[1] == PROJ_COLS

    uq = mla_w_uq[l].reshape(MLA_Q_LORA, MLA_HEADS, MLA_NOPE + MLA_ROPE)
    uq_rope = jnp.pad(uq[:, :, MLA_NOPE:], ((0, 0), (0, 0), (0, LANES - MLA_ROPE)))
    w_uq = jnp.concatenate([uq[:, :, :MLA_NOPE].reshape(MLA_Q_LORA, -1),
                            uq_rope.reshape(MLA_Q_LORA, -1)], axis=1).astype(BF16)
    pad64 = lambda v: jnp.pad(v, (0, LANES - MLA_ROPE)).reshape(1, LANES)
    lq1, lk1, lq2, lk2 = diff_lambda[l]
    lam_init = 0.8 - 0.6 * math.exp(-0.3 * l)
    lam = jnp.exp(jnp.sum(lq1 * lk1)) - jnp.exp(jnp.sum(lq2 * lk2)) + lam_init
    bias_row = jnp.pad(jnp.concatenate([ml_gate_bias[l, 0], ml_gate_bias[l, 1]]),
                       (0, LANES - 2 * ML_HEADS)).reshape(1, LANES)
    seg = jnp.arange(LANES) // DIFF_DH
    return dict(
        w_in=w_in_r, w_uq=w_uq, bias_row=bias_row, bias2=ml_gate_bias[l],
        gq_nope=mla_q_gain[l, :MLA_NOPE].reshape(1, LANES), gq_rope=pad64(mla_q_gain[l, MLA_NOPE:]),
        mla_kr_gain=pad64(mla_kr_gain[l]),
        w_uk_t=jnp.transpose(mla_w_uk[l], (1, 2, 0)).astype(BF16),
        w_uv=jnp.transpose(mla_w_uv[l], (1, 0, 2)).astype(BF16),
        w_uv_t=jnp.transpose(mla_w_uv[l], (1, 2, 0)).astype(BF16),
        diff_q_gain=jnp.tile(diff_q_gain[l], 2).reshape(1, LANES),
        diff_k_gain=jnp.tile(diff_k_gain[l], 2).reshape(1, LANES),
        seg_ones=(seg[:, None] == seg[None, :]).astype(F32),
        lam=lam.reshape(1).astype(F32), lam_init=lam_init)


def _token_mixer(l, x_all, dims, caches, states, page_table, rel_bias_table, wp, norm_mix_l,
                 ml_out_norm_l, mla_q_norm_l, mla_kv_norm_l, diff_out_norm_l, w_branch, w_out_l, tables):
    nb, seq, ns = dims
    mp = nb * seq
    cos, sin, bias_tiles_idx, bias_past_idx = tables
    cache_mla_c, cache_mla_kr, cache_diff_k, cache_diff_v = caches
    state_c, state_n, state_m = states

    m_all = x_all.shape[0]
    tm_big, tm_mid, tm_small = _tile(m_all, 1664, 640, 320), _tile(m_all, 640, 320), _tile(m_all, 320)
    h = rmsnorm_rows(x_all, norm_mix_l, tm_small)
    proj = matmul(h, wp["w_in"][None], 0, tm_big, 512, D_MODEL)

    wp = dict(wp, mla_q_norm=mla_q_norm_l.reshape(1, -1), mla_kv_norm=mla_kv_norm_l.reshape(1, -1))
    ckv, kr, dk, kcat, dkb, dvb, qm, dqn = mixer_prep(proj, cos, sin, wp, PREP_TM)
    dv = proj[:, COL_DV:COL_DV + 2 * DIFF_DH]
    kr = kr[:, :MLA_ROPE]

    hml_p, pc, pn, pm = mlstm_prompt(proj, wp["bias_row"], ml_out_norm_l, nb, seq, ML_CHUNK)
    ps = proj[mp:]
    heads_last = lambda a: jnp.transpose(a.reshape(ns, ML_HEADS, ML_DK), (0, 2, 1))
    qt = heads_last(ps[:, COL_MLQ:COL_MLQ + ML_HEADS * ML_DK])
    kt = heads_last(ps[:, COL_MLK:COL_MLK + ML_HEADS * ML_DK])
    ig = ps[:, COL_IF:COL_IF + ML_HEADS].reshape(ns, 1, ML_HEADS)
    fg = ps[:, COL_IF + ML_HEADS:COL_IF + 2 * ML_HEADS].reshape(ns, 1, ML_HEADS)
    hml_s, sc, sn, sm = mlstm_sample(proj, mp, qt, kt, ig, fg, wp["bias2"], ml_out_norm_l,
                                     state_c, state_n, state_m, l, SAMPLE_G)
    h_ml = jnp.concatenate([hml_p, hml_s.astype(BF16)], axis=0)

    hmla_p = mla_prompt(qm, kcat, wp["w_uv_t"], nb, seq, ATT_T)
    q_s = jnp.transpose(qm[:, mp:], (1, 0, 2))
    hmla_s = mla_decode(page_table, q_s, kcat[mp:].reshape(ns, 1, MLA_KW), wp["w_uv"],
                        cache_mla_c, cache_mla_kr, l, DEC_PAGES)
    h_mla = jnp.concatenate([hmla_p, hmla_s.reshape(ns, MLA_W).astype(BF16)], axis=0)

    out_scale = 1.0 - wp["lam_init"]
    table = rel_bias_table.astype(F32)
    bias_tiles = jnp.transpose(table[bias_tiles_idx], (0, 3, 1, 2))
    far = table[REL_BUCKETS - 1]
    bias_past = jnp.transpose(table[bias_past_idx])
    hdiff_p = diff_prompt(dqn, dkb, dvb, bias_tiles, far, wp["lam"], diff_out_norm_l, out_scale,
                          nb, seq, ATT_T)
    hdiff_s = diff_decode(page_table, wp["lam"], jnp.transpose(dqn[:, mp:], (1, 0, 2)),
                          dkb[mp:].reshape(ns, 1, LANES), dvb[mp:].reshape(ns, 1, LANES),
                          bias_past[:, -1:], bias_past[:, :-1], diff_out_norm_l, out_scale,
                          cache_diff_k, cache_diff_v, l, DEC_PAGES)
    h_diff = jnp.concatenate([hdiff_p, hdiff_s.reshape(ns, DIFF_W).astype(BF16)], axis=0)

    merged = gated_merge([h_ml, h_mla, h_diff], w_branch, l, proj, tm_mid, 256)
    x_new = matmul(merged, w_out_l, l, tm_mid, 512, D_MODEL, res=x_all)

    split = lambda a: (a[:mp].reshape(nb, seq, -1), a[mp:].reshape(ns, 1, -1))
    ckv_p, ckv_s = split(ckv)
    kr_p, kr_s = split(kr)
    dk_p, dk_s = split(dk)
    dv_p, dv_s = split(dv)
    st_p = (ckv_p, kr_p, dk_p, dv_p, pc, pn.reshape(nb, ML_HEADS, ML_DK), pm[:, :, 0, 0])
    st_s = (ckv_s, kr_s, dk_s, dv_s, sc, sn, sm.reshape(ns, ML_HEADS))
    return x_new, st_p, st_s


def kernel(x_prompt, x_sample, cache_mla_c, cache_mla_kr, cache_diff_k, cache_diff_v, state_mlstm_c, state_mlstm_n, state_mlstm_m, page_table, rel_bias_table, norm_mix, norm_ffn, w_in, ml_gate_bias, ml_out_norm, mla_q_norm, mla_w_uq, mla_q_gain, mla_kv_norm, mla_kr_gain, mla_w_uk, mla_w_uv, diff_q_gain, diff_k_gain, diff_lambda, diff_out_norm, w_ml_o, w_mla_o, w_diff_o, w_out, ffn_w_gate, ffn_w_up, ffn_w_down, moe_router, moe_w_gate, moe_w_up, moe_w_down):
    nb, seq, d = x_prompt.shape
    ns, dec_seq, _ = x_sample.shape
    assert dec_seq == 1 and d == D_MODEL
    depth = w_in.shape[0]
    n_pages = page_table.shape[1]
    past = n_pages * cache_mla_c.shape[2]
    mp = nb * seq
    x_all = jnp.concatenate([x_prompt.reshape(mp, d), x_sample.reshape(ns, d)], axis=0)

    pos = jnp.concatenate([jnp.tile(jnp.arange(seq, dtype=jnp.int32), nb),
                           jnp.full((ns,), past, jnp.int32)])
    inv = ROPE_THETA ** (-jnp.arange(0, MLA_ROPE, 2, dtype=F32) / MLA_ROPE)
    ang = pos.astype(F32)[:, None] * inv[None, :]
    zpad = jnp.zeros((mp + ns, LANES - MLA_ROPE), F32)
    cos = jnp.concatenate([jnp.cos(ang), jnp.cos(ang), zpad], axis=1)
    sin = jnp.concatenate([-jnp.sin(ang), jnp.sin(ang), zpad], axis=1)
    t_i = jnp.arange(ATT_T, dtype=jnp.int32)
    delta = t_i[None, :] - t_i[:, None]
    bias_tiles_idx = jnp.stack([_rel_bucket(delta), _rel_bucket(delta + ATT_T)])
    cache_mla_kr_t = jnp.swapaxes(cache_mla_kr, 2, 3)
    bias_past_idx = _rel_bucket(past - jnp.arange(past + 1, dtype=jnp.int32))
    tables = (cos, sin, bias_tiles_idx, bias_past_idx)

    per_p, per_s = [], []
    for l in range(depth):
        wp = _layer_weights(l, w_in, ml_gate_bias, mla_w_uq, mla_q_gain, mla_kr_gain, mla_w_uk, mla_w_uv,
                            diff_q_gain, diff_k_gain, diff_lambda)
        x_all, st_p, st_s = _token_mixer(
            l, x_all, (nb, seq, ns), (cache_mla_c, cache_mla_kr_t, cache_diff_k, cache_diff_v),
            (state_mlstm_c, state_mlstm_n, state_mlstm_m), page_table, rel_bias_table, wp, norm_mix[l],
            ml_out_norm[l], mla_q_norm[l], mla_kv_norm[l], diff_out_norm[l], [w_ml_o, w_mla_o, w_diff_o],
            w_out, tables)
        per_p.append(st_p)
        per_s.append(st_s)

        i = l // 2
        m_all = x_all.shape[0]
        tm_big, tm_mid, tm_small = _tile(m_all, 1664, 640, 320), _tile(m_all, 640, 320), _tile(m_all, 320)
        hf = rmsnorm_rows(x_all, norm_ffn[l], tm_small)
        if l % 2 == 0:
            act = swiglu_up(hf, ffn_w_gate[i:i + 1], ffn_w_up[i:i + 1], tm_big, 256)
            f = act.shape[1]
            x_all = matmul(act, ffn_w_down, i, tm_mid, 512, f // 2, res=x_all)
        else:
            router_pad = jnp.pad(moe_router[i], ((0, 0), (0, LANES - N_EXPERTS)))
            gates = moe_gates(x_all, norm_ffn[l], router_pad, tm_small)
            act = swiglu_up(hf, moe_w_gate[i], moe_w_up[i], tm_big, 256, gates=gates)
            wd = moe_w_down.reshape(moe_w_down.shape[0], -1, d)
            x_all = matmul(act, wd, i, tm_big, 512, 2048, res=x_all)

    outs_p = [jnp.stack([st[j] for st in per_p]) for j in range(7)]
    outs_s = [jnp.stack([st[j] for st in per_s]) for j in range(7)]
    y_prompt = x_all[:mp].reshape(nb, seq, d)
    y_sample = x_all[mp:].reshape(ns, 1, d)
    return (y_prompt, y_sample, *outs_p, *outs_s)
```

```python
import functools
import math

import jax
import jax.numpy as jnp
from jax import lax
from jax.experimental import pallas as pl
from jax.experimental.pallas import tpu as pltpu

F32 = jnp.float32
BF16 = jnp.bfloat16
HIGHEST = lax.Precision.HIGHEST

LANES = 128
SUBLANES = 8
VMEM_LIMIT_BYTES = 56 * 1024 * 1024

D_MODEL = 4096
ML_HEADS, ML_DK, ML_DV = 8, 128, 256
MLA_HEADS, MLA_NOPE, MLA_ROPE, MLA_DV = 16, 128, 64, 128
MLA_Q_LORA, MLA_KV_LORA = 768, 256
ROPE_THETA = 10000.0
DIFF_HEADS, DIFF_DH = 16, 64
REL_BUCKETS, REL_MAX_DIST = 32, 128
N_EXPERTS, TOP_K = 8, 2
NORM_EPS = 1e-6
MLA_SCALE = (MLA_NOPE + MLA_ROPE) ** -0.5
DIFF_SCALE = DIFF_DH ** -0.5
ML_W = ML_HEADS * ML_DV
MLA_W = MLA_HEADS * MLA_DV
DIFF_W = DIFF_HEADS * 2 * DIFF_DH
MLA_KW = MLA_KV_LORA + 2 * MLA_ROPE

COL_GATE = 0
COL_MLQ = COL_GATE + 3 * D_MODEL
COL_MLK = COL_MLQ + ML_HEADS * ML_DK
COL_MLV = COL_MLK + ML_HEADS * ML_DK
COL_MLO = COL_MLV + ML_W
COL_DQ = COL_MLO + ML_W
COL_QA = COL_DQ + DIFF_W
COL_DK = COL_QA + MLA_Q_LORA + MLA_KV_LORA
COL_DV = COL_DK + 2 * DIFF_DH
COL_KR = COL_DV + 2 * DIFF_DH
COL_IF = COL_KR + LANES
PROJ_COLS = COL_IF + LANES

ML_CHUNK = 256
ATT_T = 256
DEC_PAGES = 32
PREP_TM = 320
SAMPLE_G = 8


def _tile(n, *candidates):
    return next(c for c in candidates if n % c == 0)


def _params(sem):
    return pltpu.CompilerParams(dimension_semantics=sem, vmem_limit_bytes=VMEM_LIMIT_BYTES)


def _dot(a, b):
    return jnp.dot(a, b, preferred_element_type=F32)


def _dot_nt(a, b, precision=None):
    return lax.dot_general(a, b, (((1,), (1,)), ((), ())), precision=precision,
                           preferred_element_type=F32)


def _rmsnorm_kernel(x_ref, g_ref, o_ref):
    x = x_ref[...]
    y = x * lax.rsqrt(jnp.mean(x * x, axis=-1, keepdims=True) + NORM_EPS)
    o_ref[...] = (y * g_ref[...]).astype(o_ref.dtype)


def rmsnorm_rows(x, g, tm, out_dtype=BF16):
    m, d = x.shape
    return pl.pallas_call(
        _rmsnorm_kernel, grid=(m // tm,),
        in_specs=[pl.BlockSpec((tm, d), lambda i: (i, 0)), pl.BlockSpec((1, d), lambda i: (0, 0))],
        out_specs=pl.BlockSpec((tm, d), lambda i: (i, 0)),
        out_shape=jax.ShapeDtypeStruct((m, d), out_dtype),
        compiler_params=_params(("parallel",)), name="rmsnorm_rows")(x, g.reshape(1, d))


def _matmul_kernel(*refs, nk, has_res):
    if has_res:
        x_ref, w_ref, r_ref, o_ref = refs[:4]
    else:
        x_ref, w_ref, o_ref = refs[:3]
        r_ref = None

    def partial_product():
        return _dot(x_ref[...], w_ref[0].astype(BF16))

    def finish(acc):
        if has_res:
            acc = acc + r_ref[...]
        o_ref[...] = acc.astype(o_ref.dtype)

    if nk == 1:
        finish(partial_product())
        return
    acc_ref = refs[-1]
    k = pl.program_id(2)

    @pl.when(k == 0)
    def _():
        acc_ref[...] = jnp.zeros_like(acc_ref)

    acc_ref[...] += partial_product()

    @pl.when(k == nk - 1)
    def _():
        finish(acc_ref[...])


def matmul(x, w, lead, tm, tn, tk, res=None, out_dtype=F32):
    m, kd = x.shape
    n = w.shape[2]
    nk = kd // tk
    in_specs = [pl.BlockSpec((tm, tk), lambda i, j, k: (i, k)),
                pl.BlockSpec((1, tk, tn), lambda i, j, k: (lead, k, j))]
    args = [x, w]
    if res is not None:
        in_specs.append(pl.BlockSpec((tm, tn), lambda i, j, k: (i, j)))
        args.append(res)
    return pl.pallas_call(
        functools.partial(_matmul_kernel, nk=nk, has_res=res is not None),
        grid=(m // tm, n // tn, nk), in_specs=in_specs,
        out_specs=pl.BlockSpec((tm, tn), lambda i, j, k: (i, j)),
        out_shape=jax.ShapeDtypeStruct((m, n), out_dtype),
        scratch_shapes=[pltpu.VMEM((tm, tn), F32)] if nk > 1 else [],
        compiler_params=_params(("parallel", "parallel", "arbitrary")), name="matmul")(*args)


def _seg_rmsnorm(x, bd, gain, width):
    ssq = lax.dot_general(x * x, bd, (((1,), (0,)), ((), ())), precision=HIGHEST,
                          preferred_element_type=F32)
    return x * lax.rsqrt(ssq * (1.0 / width) + NORM_EPS) * gain


def _rope_lo64(x, cos, sin):
    swapped = pltpu.roll(x, MLA_ROPE // 2, 1) + pltpu.roll(x, LANES - MLA_ROPE // 2, 1)
    return x * cos + swapped * sin


def _prep_kernel(qa_ref, dq_ref, dk_ref, dv_ref, kr_ref, cos_ref, sin_ref,
                 qn_ref, wuq_ref, gqn_ref, gqr_ref, gkv_ref, gkr_ref, wuk_ref,
                 gdq_ref, gdk_ref, bd_ref,
                 ckv_o, kr_o, dk_o, kcat_o, dkb_o, dvb_o, qm_o, dqn_o):
    cos = cos_ref[...]
    sin = sin_ref[...]
    lane = lax.broadcasted_iota(jnp.int32, cos.shape, 1)
    lo64 = lane < MLA_ROPE

    ckv = qa_ref[:, MLA_Q_LORA:]
    ckv = ckv * lax.rsqrt(jnp.mean(ckv * ckv, axis=-1, keepdims=True) + NORM_EPS) * gkv_ref[...]
    ckv_o[...] = ckv
    kr = jnp.where(lo64, kr_ref[...], 0.0)
    kr = kr * lax.rsqrt(jnp.sum(kr * kr, axis=-1, keepdims=True) * (1.0 / MLA_ROPE) + NORM_EPS)
    kr = _rope_lo64(kr * gkr_ref[...], cos, sin)
    kr_o[...] = kr
    kcat_o[:, :MLA_KV_LORA] = ckv.astype(BF16)
    kcat_o[:, MLA_KV_LORA:] = kr.astype(BF16)

    bd = bd_ref[...]
    dk = _seg_rmsnorm(dk_ref[...], bd, gdk_ref[...], DIFF_DH)
    dk_o[...] = dk
    dkb_o[...] = dk.astype(BF16)
    dvb_o[...] = dv_ref[...].astype(BF16)
    gdq = gdq_ref[...]
    for h in range(DIFF_HEADS):
        sl = slice(h * LANES, (h + 1) * LANES)
        dqn_o[h] = (_seg_rmsnorm(dq_ref[:, sl], bd, gdq, DIFF_DH) * DIFF_SCALE).astype(BF16)

    qa = qa_ref[:, :MLA_Q_LORA]
    qa = qa * lax.rsqrt(jnp.mean(qa * qa, axis=-1, keepdims=True) + NORM_EPS) * qn_ref[...]
    qf = _dot(qa.astype(BF16), wuq_ref[...])
    gqn = gqn_ref[...]
    gqr = gqr_ref[...]
    nope_w = MLA_HEADS * MLA_NOPE
    for h in range(MLA_HEADS):
        nope = qf[:, h * MLA_NOPE:(h + 1) * MLA_NOPE]
        rope = qf[:, nope_w + h * LANES:nope_w + (h + 1) * LANES]
        ssq = (jnp.sum(nope * nope, axis=-1, keepdims=True)
               + jnp.sum(rope * rope, axis=-1, keepdims=True))
        inv = lax.rsqrt(ssq * (1.0 / (MLA_NOPE + MLA_ROPE)) + NORM_EPS)
        rope = _rope_lo64(rope * inv * gqr, cos, sin)
        q_lat = _dot((nope * inv * gqn).astype(BF16), wuk_ref[h])
        qm_o[h, :, :MLA_KV_LORA] = (q_lat * MLA_SCALE).astype(BF16)
        qm_o[h, :, MLA_KV_LORA:] = (rope * MLA_SCALE).astype(BF16)


def mixer_prep(proj, cos, sin, wp, tm):
    m = proj.shape[0]
    row = lambda w, cb: pl.BlockSpec((tm, w), lambda i, cb=cb: (i, cb))
    full = lambda a: pl.BlockSpec(a.shape, lambda i, nd=a.ndim: (0,) * nd)
    qa_w = MLA_Q_LORA + MLA_KV_LORA
    consts = [wp["mla_q_norm"], wp["w_uq"], wp["gq_nope"], wp["gq_rope"], wp["mla_kv_norm"],
              wp["mla_kr_gain"], wp["w_uk_t"], wp["diff_q_gain"], wp["diff_k_gain"], wp["seg_ones"]]
    outs = [((m, MLA_KV_LORA), F32), ((m, LANES), F32), ((m, LANES), F32),
            ((m, MLA_KW), BF16), ((m, LANES), BF16), ((m, LANES), BF16),
            ((MLA_HEADS, m, MLA_KW), BF16), ((DIFF_HEADS, m, LANES), BF16)]
    out_specs = [pl.BlockSpec((tm, s[-1]), lambda i: (i, 0)) if len(s) == 2
                 else pl.BlockSpec((s[0], tm, s[-1]), lambda i: (0, i, 0)) for s, _ in outs]
    return pl.pallas_call(
        _prep_kernel, grid=(m // tm,),
        in_specs=[row(qa_w, COL_QA // qa_w), row(DIFF_W, COL_DQ // DIFF_W), row(LANES, COL_DK // LANES),
                  row(LANES, COL_DV // LANES), row(LANES, COL_KR // LANES),
                  pl.BlockSpec((tm, LANES), lambda i: (i, 0)), pl.BlockSpec((tm, LANES), lambda i: (i, 0))]
                 + [full(a) for a in consts],
        out_specs=out_specs,
        out_shape=[jax.ShapeDtypeStruct(s, dt) for s, dt in outs],
        compiler_params=_params(("parallel",)), name="mixer_prep",
    )(proj, proj, proj, proj, proj, cos, sin, *consts)


def _log_sigmoid(x):
    return jnp.minimum(x, 0.0) - jnp.log(1.0 + jnp.exp(-jnp.abs(x)))


def _mlstm_prompt_kernel(q_ref, k_ref, v_ref, o_ref, if_ref, bias_ref, gain_ref, tri_ref,
                         h_o, c_o, n_o, m_o, c_sc, n_sc, m_sc, *, nc, chunk):
    ci = pl.program_id(1)

    @pl.when(ci == 0)
    def _():
        c_sc[...] = jnp.zeros_like(c_sc)
        n_sc[...] = jnp.zeros_like(n_sc)
        m_sc[...] = jnp.zeros_like(m_sc)

    g2 = if_ref[...] + bias_ref[...]
    cum = lax.dot_general(tri_ref[...], _log_sigmoid(g2), (((1,), (0,)), ((), ())), precision=HIGHEST,
                          preferred_element_type=F32)
    sub8 = lax.broadcasted_iota(jnp.int32, (ML_HEADS, LANES), 0)
    lane8 = lax.broadcasted_iota(jnp.int32, (ML_HEADS, LANES), 1)
    ig_rows = _dot_nt((lane8 == sub8).astype(F32), g2, precision=HIGHEST)
    b_rows = _dot_nt((lane8 == sub8 + ML_HEADS).astype(F32), cum, precision=HIGHEST)
    r_i = lax.broadcasted_iota(jnp.int32, (chunk, chunk), 0)
    c_i = lax.broadcasted_iota(jnp.int32, (chunk, chunk), 1)
    causal = c_i <= r_i

    for hd in range(ML_HEADS):
        ig_col = g2[:, hd:hd + 1]
        b_col = cum[:, ML_HEADS + hd:ML_HEADS + hd + 1]
        ig_row = ig_rows[hd:hd + 1]
        b_row = b_rows[hd:hd + 1]
        b_last = b_row[:, chunk - 1:chunk]

        m_prev = m_sc[hd, :, 0:1]
        d = jnp.where(causal, b_col - b_row + ig_row, -jnp.inf)
        inter = b_col + m_prev
        m_t = jnp.maximum(inter, jnp.max(d, axis=1, keepdims=True))
        q = q_ref[:, hd * ML_DK:(hd + 1) * ML_DK]
        qb = q.astype(BF16)
        ks = k_ref[:, hd * ML_DK:(hd + 1) * ML_DK] * (ML_DK ** -0.5)
        v = v_ref[:, hd * ML_DV:(hd + 1) * ML_DV].astype(BF16)
        s = _dot_nt(qb, ks.astype(BF16)) * jnp.exp(d - m_t)
        w_inter = jnp.exp(inter - m_t)
        c_prev = c_sc[hd]
        n_prev = n_sc[hd]
        num = _dot(s.astype(BF16), v) + w_inter * _dot(qb, c_prev.astype(BF16))
        den = jnp.sum(s, axis=1, keepdims=True) + w_inter * jnp.sum(q * n_prev, axis=1, keepdims=True)
        hh = num / jnp.maximum(jnp.abs(den), jnp.exp(-m_t))
        hn = hh * lax.rsqrt(jnp.mean(hh * hh, axis=-1, keepdims=True) + NORM_EPS) * gain_ref[hd]
        og = o_ref[:, hd * ML_DV:(hd + 1) * ML_DV]
        h_o[:, hd * ML_DV:(hd + 1) * ML_DV] = (hn * jax.nn.sigmoid(og)).astype(h_o.dtype)

        g_col = b_last - b_col + ig_col
        m_new = jnp.maximum(b_last + m_prev, jnp.max(g_col, axis=0, keepdims=True))
        kw = ks * jnp.exp(g_col - m_new)
        decay = jnp.exp(b_last + m_prev - m_new)
        c_sc[hd] = decay * c_prev + _dot(jnp.transpose(kw).astype(BF16), v)
        n_sc[hd] = decay * n_prev + jnp.sum(kw, axis=0, keepdims=True)
        m_sc[hd] = jnp.broadcast_to(m_new, (1, LANES))

    @pl.when(ci == nc - 1)
    def _():
        c_o[0] = c_sc[...]
        n_o[0] = n_sc[...]
        m_o[0] = m_sc[...]


def mlstm_prompt(proj, bias_row, out_gain, nb, seq, chunk):
    nc = seq // chunk
    tri = jnp.tril(jnp.ones((chunk, chunk), F32))
    rowblk = lambda b, c: b * nc + c
    qk_w = ML_HEADS * ML_DK
    return pl.pallas_call(
        functools.partial(_mlstm_prompt_kernel, nc=nc, chunk=chunk),
        grid=(nb, nc),
        in_specs=[
            pl.BlockSpec((chunk, qk_w), lambda b, c: (rowblk(b, c), COL_MLQ // qk_w)),
            pl.BlockSpec((chunk, qk_w), lambda b, c: (rowblk(b, c), COL_MLK // qk_w)),
            pl.BlockSpec((chunk, ML_W), lambda b, c: (rowblk(b, c), COL_MLV // ML_W)),
            pl.BlockSpec((chunk, ML_W), lambda b, c: (rowblk(b, c), COL_MLO // ML_W)),
            pl.BlockSpec((chunk, LANES), lambda b, c: (rowblk(b, c), COL_IF // LANES)),
            pl.BlockSpec((1, LANES), lambda b, c: (0, 0)),
            pl.BlockSpec((ML_HEADS, 1, ML_DV), lambda b, c: (0, 0, 0)),
            pl.BlockSpec((chunk, chunk), lambda b, c: (0, 0)),
        ],
        out_specs=[
            pl.BlockSpec((chunk, ML_W), lambda b, c: (rowblk(b, c), 0)),
            pl.BlockSpec((1, ML_HEADS, ML_DK, ML_DV), lambda b, c: (b, 0, 0, 0)),
            pl.BlockSpec((1, ML_HEADS, 1, ML_DK), lambda b, c: (b, 0, 0, 0)),
            pl.BlockSpec((1, ML_HEADS, 1, LANES), lambda b, c: (b, 0, 0, 0)),
        ],
        out_shape=[jax.ShapeDtypeStruct((nb * seq, ML_W), BF16),
                   jax.ShapeDtypeStruct((nb, ML_HEADS, ML_DK, ML_DV), F32),
                   jax.ShapeDtypeStruct((nb, ML_HEADS, 1, ML_DK), F32),
                   jax.ShapeDtypeStruct((nb, ML_HEADS, 1, LANES), F32)],
        scratch_shapes=[pltpu.VMEM((ML_HEADS, ML_DK, ML_DV), F32), pltpu.VMEM((ML_HEADS, 1, ML_DK), F32),
                        pltpu.VMEM((ML_HEADS, 1, LANES), F32)],
        compiler_params=_params(("parallel", "arbitrary")), name="mlstm_prompt",
    )(proj, proj, proj, proj, proj, bias_row, out_gain.reshape(ML_HEADS, 1, ML_DV), tri)


def _mlstm_sample_kernel(q_ref, k_ref, v_ref, o_ref, qt_ref, kt_ref, ig_ref, fg_ref, bias_ref,
                         gain_ref, c_ref, n_ref, m_ref, h_o, c_o, n_o, m_o, *, g_seqs):
    lane8 = lax.broadcasted_iota(jnp.int32, (1, ML_HEADS), 1)
    for g in range(g_seqs):
        ig_all = ig_ref[g] + bias_ref[0:1, :]
        lf_all = _log_sigmoid(fg_ref[g] + bias_ref[1:2, :])
        m_all = m_ref[0, g]
        m_out = jnp.zeros((1, ML_HEADS), F32)
        for h in range(ML_HEADS):
            ig = ig_all[:, h:h + 1]
            inter = lf_all[:, h:h + 1] + m_all[:, h:h + 1]
            m_t = jnp.maximum(inter, ig)
            q = q_ref[g:g + 1, h * ML_DK:(h + 1) * ML_DK]
            k = k_ref[g:g + 1, h * ML_DK:(h + 1) * ML_DK] * (ML_DK ** -0.5)
            v = v_ref[g:g + 1, h * ML_DV:(h + 1) * ML_DV]
            q_col = qt_ref[g, :, h:h + 1]
            k_col = kt_ref[g, :, h:h + 1] * (ML_DK ** -0.5)
            c_prev = c_ref[0, g, h]
            n_prev = n_ref[0, g, h:h + 1, :]
            w_new = jnp.exp(ig - m_t)
            w_old = jnp.exp(inter - m_t)
            s = jnp.sum(q * k, axis=1, keepdims=True) * w_new
            num = s * v + w_old * jnp.sum(q_col * c_prev, axis=0, keepdims=True)
            den = s + w_old * jnp.sum(q * n_prev, axis=1, keepdims=True)
            hh = num / jnp.maximum(jnp.abs(den), jnp.exp(-m_t))
            hn = hh * lax.rsqrt(jnp.mean(hh * hh, axis=-1, keepdims=True) + NORM_EPS) * gain_ref[h]
            og = o_ref[g:g + 1, h * ML_DV:(h + 1) * ML_DV]
            h_o[g:g + 1, h * ML_DV:(h + 1) * ML_DV] = hn * jax.nn.sigmoid(og)
            c_o[g, h] = w_old * c_prev + (w_new * k_col) * v
            n_o[g, h:h + 1, :] = w_old * n_prev + w_new * k
            m_out = jnp.where(lane8 == h, m_t, m_out)
        m_o[g] = m_out


def mlstm_sample(proj, row0, qt, kt, ig, fg, bias2, out_gain, state_c, state_n, state_m, layer, g_seqs):
    n = state_c.shape[1]
    rb = row0 // g_seqs
    return pl.pallas_call(
        functools.partial(_mlstm_sample_kernel, g_seqs=g_seqs),
        grid=(n // g_seqs,),
        in_specs=[
            pl.BlockSpec((g_seqs, ML_HEADS * ML_DK), lambda i: (rb + i, COL_MLQ // (ML_HEADS * ML_DK))),
            pl.BlockSpec((g_seqs, ML_HEADS * ML_DK), lambda i: (rb + i, COL_MLK // (ML_HEADS * ML_DK))),
            pl.BlockSpec((g_seqs, ML_W), lambda i: (rb + i, COL_MLV // ML_W)),
            pl.BlockSpec((g_seqs, ML_W), lambda i: (rb + i, COL_MLO // ML_W)),
            pl.BlockSpec((g_seqs, ML_DK, ML_HEADS), lambda i: (i, 0, 0)),
            pl.BlockSpec((g_seqs, ML_DK, ML_HEADS), lambda i: (i, 0, 0)),
            pl.BlockSpec((g_seqs, 1, ML_HEADS), lambda i: (i, 0, 0)),
            pl.BlockSpec((g_seqs, 1, ML_HEADS), lambda i: (i, 0, 0)),
            pl.BlockSpec((2, ML_HEADS), lambda i: (0, 0)),
            pl.BlockSpec((ML_HEADS, 1, ML_DV), lambda i: (0, 0, 0)),
            pl.BlockSpec((1, g_seqs, ML_HEADS, ML_DK, ML_DV), lambda i: (layer, i, 0, 0, 0)),
            pl.BlockSpec((1, g_seqs, ML_HEADS, ML_DK), lambda i: (layer, i, 0, 0)),
            pl.BlockSpec((1, g_seqs, 1, ML_HEADS), lambda i: (layer, i, 0, 0)),
        ],
        out_specs=[
            pl.BlockSpec((g_seqs, ML_W), lambda i: (i, 0)),
            pl.BlockSpec((g_seqs, ML_HEADS, ML_DK, ML_DV), lambda i: (i, 0, 0, 0)),
            pl.BlockSpec((g_seqs, ML_HEADS, ML_DK), lambda i: (i, 0, 0)),
            pl.BlockSpec((g_seqs, 1, ML_HEADS), lambda i: (i, 0, 0)),
        ],
        out_shape=[jax.ShapeDtypeStruct((n, ML_W), F32),
                   jax.ShapeDtypeStruct((n, ML_HEADS, ML_DK, ML_DV), F32),
                   jax.ShapeDtypeStruct((n, ML_HEADS, ML_DK), F32),
                   jax.ShapeDtypeStruct((n, 1, ML_HEADS), F32)],
        compiler_params=_params(("parallel",)), name="mlstm_sample",
    )(proj, proj, proj, proj, qt, kt, ig, fg, bias2, out_gain.reshape(ML_HEADS, 1, ML_DV),
      state_c, state_n, state_m.reshape(state_m.shape[0], n, 1, ML_HEADS))


ATT_GROUP = 8


def _softmax_group_t(s_ts, v_t, m_sc, l_sc, acc_sc, g):
    n = len(s_ts)
    m_prev = [m_sc[g, j] for j in range(n)]
    l_prev = [l_sc[g, j] for j in range(n)]
    acc_prev = [acc_sc[g, j] for j in range(n)]
    m_new = [jnp.maximum(m_prev[j], jnp.max(s_ts[j], axis=0, keepdims=True)) for j in range(n)]
    p = [jnp.exp(s_ts[j] - m_new[j]) for j in range(n)]
    alpha = [jnp.exp(m_prev[j] - m_new[j]) for j in range(n)]
    pv = [_dot(v_t, p[j].astype(BF16)) for j in range(n)]
    for j in range(n):
        l_sc[g, j] = alpha[j] * l_prev[j] + jnp.sum(p[j], axis=0, keepdims=True)
        acc_sc[g, j] = alpha[j] * acc_prev[j] + pv[j]
        m_sc[g, j] = m_new[j]


def _key_visible(t, qi, ki):
    k_pos = lax.broadcasted_iota(jnp.int32, (t, t), 0) + ki * t
    q_pos = lax.broadcasted_iota(jnp.int32, (t, t), 1) + qi * t
    return k_pos <= q_pos


def _mla_prompt_kernel(q_ref, k_ref, wuvt_ref, o_ref, m_sc, l_sc, acc_sc, *, t):
    qi = pl.program_id(1)
    ki = pl.program_id(2)

    @pl.when(ki == 0)
    def _():
        m_sc[...] = jnp.full_like(m_sc, -jnp.inf)
        l_sc[...] = jnp.zeros_like(l_sc)
        acc_sc[...] = jnp.zeros_like(acc_sc)

    def sweep(masked):
        k = k_ref[...]
        c_t = jnp.transpose(k[:, :MLA_KV_LORA].astype(F32)).astype(BF16)
        visible = _key_visible(t, qi, ki) if masked else None

        def heads(g, carry):
            s_ts = [_dot_nt(k, q_ref[g * ATT_GROUP + j]) for j in range(ATT_GROUP)]
            if masked:
                s_ts = [jnp.where(visible, s_t, -jnp.inf) for s_t in s_ts]
            _softmax_group_t(s_ts, c_t, m_sc, l_sc, acc_sc, g)
            return carry

        lax.fori_loop(0, MLA_HEADS // ATT_GROUP, heads, 0)

    pl.when(ki < qi)(lambda: sweep(False))

    @pl.when(ki == qi)
    def _():
        sweep(True)
        for h in range(MLA_HEADS):
            g, j = divmod(h, ATT_GROUP)
            o_lat = (acc_sc[g, j] * (1.0 / l_sc[g, j])).astype(BF16)
            out_t = _dot(wuvt_ref[h], o_lat)
            o_ref[:, h * MLA_DV:(h + 1) * MLA_DV] = jnp.transpose(out_t).astype(o_ref.dtype)


def mla_prompt(qm, kcat, w_uv_t, nb, seq, t):
    nt = seq // t
    return pl.pallas_call(
        functools.partial(_mla_prompt_kernel, t=t),
        grid=(nb, nt, nt),
        in_specs=[
            pl.BlockSpec((MLA_HEADS, t, MLA_KW), lambda b, qi, ki: (0, b * nt + qi, 0)),
            pl.BlockSpec((t, MLA_KW), lambda b, qi, ki: (b * nt + jnp.minimum(ki, qi), 0)),
            pl.BlockSpec((MLA_HEADS, MLA_DV, MLA_KV_LORA), lambda b, qi, ki: (0, 0, 0)),
        ],
        out_specs=pl.BlockSpec((t, MLA_W), lambda b, qi, ki: (b * nt + qi, 0)),
        out_shape=jax.ShapeDtypeStruct((nb * seq, MLA_W), BF16),
        scratch_shapes=[pltpu.VMEM((MLA_HEADS // ATT_GROUP, ATT_GROUP, 1, t), F32),
                        pltpu.VMEM((MLA_HEADS // ATT_GROUP, ATT_GROUP, 1, t), F32),
                        pltpu.VMEM((MLA_HEADS // ATT_GROUP, ATT_GROUP, MLA_KV_LORA, t), F32)],
        compiler_params=_params(("parallel", "parallel", "arbitrary")), name="mla_prompt",
    )(qm, kcat, w_uv_t)


def _two_map_rows(q):
    lane = lax.broadcasted_iota(jnp.int32, q.shape, 1)
    zero = jnp.zeros_like(q)
    return jnp.concatenate([jnp.where(lane < DIFF_DH, q, zero), jnp.where(lane >= DIFF_DH, q, zero)], axis=0)


def _diff_prompt_kernel(far_ref, lam_ref, q_ref, k_ref, v_ref, bias_ref, gain_ref, o_ref,
                        m_sc, l_sc, acc_sc, *, t, out_scale):
    qi = pl.program_id(1)
    ki = pl.program_id(2)

    @pl.when(ki == 0)
    def _():
        m_sc[...] = jnp.full_like(m_sc, -jnp.inf)
        l_sc[...] = jnp.zeros_like(l_sc)
        acc_sc[...] = jnp.zeros_like(acc_sc)

    def sweep(kind):
        k = k_ref[...]
        lane = lax.broadcasted_iota(jnp.int32, k.shape, 1)
        zero = jnp.zeros_like(k)
        k_maps = (jnp.where(lane < DIFF_DH, k, zero), jnp.where(lane >= DIFF_DH, k, zero))
        v_t = jnp.transpose(v_ref[...].astype(F32)).astype(BF16)
        visible = _key_visible(t, qi, ki) if kind == 0 else None

        def heads(g, carry):
            s_ts = []
            for hh in range(hpg):
                h = g * hpg + hh
                q = q_ref[h]
                bias_t = far_ref[h] if kind == 2 else bias_ref[kind, h]
                for j in range(2):
                    s_t = _dot_nt(k_maps[j], q) + bias_t
                    s_ts.append(jnp.where(visible, s_t, -jnp.inf) if kind == 0 else s_t)
            _softmax_group_t(s_ts, v_t, m_sc, l_sc, acc_sc, g)
            return carry

        lax.fori_loop(0, DIFF_HEADS // hpg, heads, 0)

    hpg = ATT_GROUP // 2
    pl.when(ki == qi - 1)(lambda: sweep(1))
    pl.when(ki < qi - 1)(lambda: sweep(2))

    @pl.when(ki == qi)
    def _():
        sweep(0)
        lam = lam_ref[0]
        gain = gain_ref[...] * out_scale
        for h in range(DIFF_HEADS):
            g, j = divmod(2 * h, ATT_GROUP)
            o = (acc_sc[g, j] * (1.0 / l_sc[g, j])
                 - lam * (acc_sc[g, j + 1] * (1.0 / l_sc[g, j + 1])))
            o = o * lax.rsqrt(jnp.mean(o * o, axis=0, keepdims=True) + NORM_EPS) * gain
            o_ref[:, h * LANES:(h + 1) * LANES] = jnp.transpose(o).astype(o_ref.dtype)


def diff_prompt(dqn, dkb, dvb, bias_tiles, far, lam, out_gain, out_scale, nb, seq, t):
    nt = seq // t
    smem = pl.BlockSpec(memory_space=pltpu.SMEM)
    return pl.pallas_call(
        functools.partial(_diff_prompt_kernel, t=t, out_scale=out_scale),
        grid=(nb, nt, nt),
        in_specs=[
            smem, smem,
            pl.BlockSpec((DIFF_HEADS, t, LANES), lambda b, qi, ki: (0, b * nt + qi, 0)),
            pl.BlockSpec((t, LANES), lambda b, qi, ki: (b * nt + jnp.minimum(ki, qi), 0)),
            pl.BlockSpec((t, LANES), lambda b, qi, ki: (b * nt + jnp.minimum(ki, qi), 0)),
            pl.BlockSpec((2, DIFF_HEADS, t, t), lambda b, qi, ki: (0, 0, 0, 0)),
            pl.BlockSpec((LANES, 1), lambda b, qi, ki: (0, 0)),
        ],
        out_specs=pl.BlockSpec((t, DIFF_W), lambda b, qi, ki: (b * nt + qi, 0)),
        scratch_shapes=[pltpu.VMEM((2 * DIFF_HEADS // ATT_GROUP, ATT_GROUP, 1, t), F32),
                        pltpu.VMEM((2 * DIFF_HEADS // ATT_GROUP, ATT_GROUP, 1, t), F32),
                        pltpu.VMEM((2 * DIFF_HEADS // ATT_GROUP, ATT_GROUP, LANES, t), F32)],
        out_shape=jax.ShapeDtypeStruct((nb * seq, DIFF_W), BF16),
        compiler_params=_params(("parallel", "parallel", "arbitrary")), name="diff_prompt",
    )(far, lam, dqn, dkb, dvb, bias_tiles, out_gain.reshape(LANES, 1))


def _mla_decode_kernel(pt_ref, q_ref, knew_ref, wuv_ref, *rest, pages, nsteps):
    c_refs = rest[:pages]
    krt_refs = rest[pages:2 * pages]
    o_ref, m_sc, l_sc, acc_sc, c_buf, krt_buf = rest[2 * pages:]
    step = pl.program_id(1)
    q = q_ref[0]

    @pl.when(step == 0)
    def _():
        k_new = knew_ref[0].astype(F32)
        m_sc[...] = jnp.sum(q.astype(F32) * k_new, axis=1, keepdims=True)
        l_sc[...] = jnp.ones_like(l_sc)
        acc_sc[...] = jnp.broadcast_to(k_new[:, :MLA_KV_LORA], acc_sc.shape)

    for j in range(pages):
        c_buf[j * LANES:(j + 1) * LANES, :] = c_refs[j][0, 0].astype(BF16)
        krt_buf[:, j * LANES:(j + 1) * LANES] = krt_refs[j][0, 0].astype(BF16)
    c_all = c_buf[...]
    s = (_dot_nt(q[:, :MLA_KV_LORA], c_all)
         + _dot(q[:, MLA_KV_LORA:MLA_KV_LORA + MLA_ROPE], krt_buf[...]))
    m_prev = m_sc[...]
    m_new = jnp.maximum(m_prev, jnp.max(s, axis=1, keepdims=True))
    p = jnp.exp(s - m_new)
    alpha = jnp.exp(m_prev - m_new)
    l_sc[...] = alpha * l_sc[...] + jnp.sum(p, axis=1, keepdims=True)
    acc_sc[...] = alpha * acc_sc[...] + _dot(p.astype(BF16), c_all)
    m_sc[...] = m_new

    @pl.when(step == nsteps - 1)
    def _():
        o_lat = (acc_sc[...] / l_sc[...]).astype(BF16)
        for h in range(MLA_HEADS):
            o_ref[0, :, h * MLA_DV:(h + 1) * MLA_DV] = _dot(o_lat, wuv_ref[h])[h:h + 1]


def mla_decode(page_table, q_s, knew, w_uv, cache_c, cache_kr_t, layer, pages):
    n, n_pages = page_table.shape
    page = cache_c.shape[2]
    nsteps = n_pages // pages
    pt = page_table.reshape(-1)

    def page_spec(rows, width, j):
        return pl.BlockSpec((1, 1, rows, width),
                            lambda i, s, pt_ref, j=j: (layer, pt_ref[i * n_pages + s * pages + j], 0, 0))

    grid_spec = pltpu.PrefetchScalarGridSpec(
        num_scalar_prefetch=1, grid=(n, nsteps),
        in_specs=[pl.BlockSpec((1, MLA_HEADS, MLA_KW), lambda i, s, _: (i, 0, 0)),
                  pl.BlockSpec((1, 1, MLA_KW), lambda i, s, _: (i, 0, 0)),
                  pl.BlockSpec((MLA_HEADS, MLA_KV_LORA, MLA_DV), lambda i, s, _: (0, 0, 0))]
                 + [page_spec(page, MLA_KV_LORA, j) for j in range(pages)]
                 + [page_spec(MLA_ROPE, page, j) for j in range(pages)],
        out_specs=pl.BlockSpec((1, 1, MLA_W), lambda i, s, _: (i, 0, 0)),
        scratch_shapes=[pltpu.VMEM((MLA_HEADS, 1), F32), pltpu.VMEM((MLA_HEADS, 1), F32),
                        pltpu.VMEM((MLA_HEADS, MLA_KV_LORA), F32),
                        pltpu.VMEM((pages * page, MLA_KV_LORA), BF16),
                        pltpu.VMEM((MLA_ROPE, pages * page), BF16)])
    assert page == LANES
    return pl.pallas_call(
        functools.partial(_mla_decode_kernel, pages=pages, nsteps=nsteps),
        grid_spec=grid_spec,
        out_shape=jax.ShapeDtypeStruct((n, 1, MLA_W), F32),
        compiler_params=_params(("parallel", "arbitrary")), name="mla_decode",
    )(pt, q_s, knew, w_uv, *([cache_c] * pages), *([cache_kr_t] * pages))


def _diff_decode_kernel(pt_ref, lam_ref, q_ref, knew_ref, vnew_ref, bnew_ref, bias_ref, gain_ref,
                        *rest, pages, nsteps, out_scale):
    k_refs = rest[:pages]
    v_refs = rest[pages:2 * pages]
    o_ref, m_sc, l_sc, acc_sc, k_buf, v_buf = rest[2 * pages:]
    step = pl.program_id(1)
    nh = DIFF_HEADS
    qz = _two_map_rows(q_ref[0])

    @pl.when(step == 0)
    def _():
        b_new = bnew_ref[...]
        s_new = jnp.sum(qz.astype(F32) * knew_ref[0].astype(F32), axis=1, keepdims=True)
        m_sc[...] = s_new + jnp.concatenate([b_new, b_new], axis=0)
        l_sc[...] = jnp.ones_like(l_sc)
        acc_sc[...] = jnp.broadcast_to(vnew_ref[0].astype(F32), acc_sc.shape)

    for j in range(pages):
        k_buf[j * LANES:(j + 1) * LANES, :] = k_refs[j][0, 0].astype(BF16)
        v_buf[j * LANES:(j + 1) * LANES, :] = v_refs[j][0, 0].astype(BF16)
    bias = bias_ref[...]
    s = _dot_nt(qz, k_buf[...]) + jnp.concatenate([bias, bias], axis=0)
    m_prev = m_sc[...]
    m_new = jnp.maximum(m_prev, jnp.max(s, axis=1, keepdims=True))
    p = jnp.exp(s - m_new)
    alpha = jnp.exp(m_prev - m_new)
    l_sc[...] = alpha * l_sc[...] + jnp.sum(p, axis=1, keepdims=True)
    acc_sc[...] = alpha * acc_sc[...] + _dot(p.astype(BF16), v_buf[...])
    m_sc[...] = m_new

    @pl.when(step == nsteps - 1)
    def _():
        o = acc_sc[...] / l_sc[...]
        o = o[:nh] - lam_ref[0] * o[nh:]
        o = o * lax.rsqrt(jnp.mean(o * o, axis=-1, keepdims=True) + NORM_EPS) * gain_ref[...]
        o_ref[0] = o * out_scale


def diff_decode(page_table, lam, q_s, knew, vnew, bias_new, bias_past, out_gain, out_scale,
                cache_k, cache_v, layer, pages):
    n, n_pages = page_table.shape
    page = cache_k.shape[2]
    nsteps = n_pages // pages
    pt = page_table.reshape(-1)

    def page_spec(j):
        return pl.BlockSpec((1, 1, page, LANES),
                            lambda i, s, pt_ref, j=j: (layer, pt_ref[i * n_pages + s * pages + j], 0, 0))

    grid_spec = pltpu.PrefetchScalarGridSpec(
        num_scalar_prefetch=1, grid=(n, nsteps),
        in_specs=[pl.BlockSpec(memory_space=pltpu.SMEM),
                  pl.BlockSpec((1, DIFF_HEADS, LANES), lambda i, s, *_: (i, 0, 0)),
                  pl.BlockSpec((1, 1, LANES), lambda i, s, *_: (i, 0, 0)),
                  pl.BlockSpec((1, 1, LANES), lambda i, s, *_: (i, 0, 0)),
                  pl.BlockSpec((DIFF_HEADS, 1), lambda i, s, *_: (0, 0)),
                  pl.BlockSpec((DIFF_HEADS, pages * page), lambda i, s, *_: (0, s)),
                  pl.BlockSpec((1, LANES), lambda i, s, *_: (0, 0))]
                 + [page_spec(j) for j in range(pages)] * 2,
        out_specs=pl.BlockSpec((1, DIFF_HEADS, LANES), lambda i, s, *_: (i, 0, 0)),
        scratch_shapes=[pltpu.VMEM((2 * DIFF_HEADS, 1), F32), pltpu.VMEM((2 * DIFF_HEADS, 1), F32),
                        pltpu.VMEM((2 * DIFF_HEADS, LANES), F32),
                        pltpu.VMEM((pages * page, LANES), BF16), pltpu.VMEM((pages * page, LANES), BF16)])
    assert page == LANES
    return pl.pallas_call(
        functools.partial(_diff_decode_kernel, pages=pages, nsteps=nsteps, out_scale=out_scale),
        grid_spec=grid_spec,
        out_shape=jax.ShapeDtypeStruct((n, DIFF_HEADS, LANES), F32),
        compiler_params=_params(("parallel", "arbitrary")), name="diff_decode",
    )(pt, lam, q_s, knew, vnew, bias_new, bias_past, out_gain.reshape(1, LANES),
      *([cache_k] * pages), *([cache_v] * pages))


def _merge_kernel(h0_ref, h1_ref, h2_ref, w0_ref, w1_ref, w2_ref, g0_ref, g1_ref, g2_ref, o_ref):
    acc = jax.nn.sigmoid(g0_ref[...]) * _dot(h0_ref[...], w0_ref[0].astype(BF16))
    acc = acc + jax.nn.sigmoid(g1_ref[...]) * _dot(h1_ref[...], w1_ref[0].astype(BF16))
    acc = acc + jax.nn.sigmoid(g2_ref[...]) * _dot(h2_ref[...], w2_ref[0].astype(BF16))
    o_ref[...] = acc.astype(o_ref.dtype)


def gated_merge(hs, ws, lead, proj, tm, tn):
    m, kd = hs[0].shape
    d = ws[0].shape[2]
    nblk = d // tn
    h_spec = pl.BlockSpec((tm, kd), lambda i, j: (i, 0))
    w_spec = pl.BlockSpec((1, kd, tn), lambda i, j: (lead, 0, j))
    g_specs = [pl.BlockSpec((tm, tn), lambda i, j, b=b: (i, COL_GATE // tn + b * nblk + j)) for b in range(3)]
    return pl.pallas_call(
        _merge_kernel, grid=(m // tm, nblk),
        in_specs=[h_spec] * 3 + [w_spec] * 3 + g_specs,
        out_specs=pl.BlockSpec((tm, tn), lambda i, j: (i, j)),
        out_shape=jax.ShapeDtypeStruct((m, d), BF16),
        compiler_params=_params(("parallel", "parallel")), name="gated_merge",
    )(*hs, *ws, proj, proj, proj)


def _swiglu_up_kernel(x_ref, wg_ref, wu_ref, *rest, gated):
    x = x_ref[...]
    a = _dot(x, wg_ref[0].astype(BF16))
    act = a * jax.nn.sigmoid(a) * _dot(x, wu_ref[0].astype(BF16))
    if gated:
        gate_ref, o_ref = rest
        lane = lax.broadcasted_iota(jnp.int32, gate_ref.shape, 1)
        act = act * jnp.sum(jnp.where(lane == pl.program_id(1), gate_ref[...], 0.0), axis=1, keepdims=True)
    else:
        o_ref, = rest
    o_ref[...] = act.astype(o_ref.dtype)


def swiglu_up(x, wg, wu, tm, tn, gates=None):
    m, d = x.shape
    ne, _, f = wg.shape
    nf = f // tn
    in_specs = [pl.BlockSpec((tm, d), lambda i, e, j: (i, 0)),
                pl.BlockSpec((1, d, tn), lambda i, e, j: (e, 0, j)),
                pl.BlockSpec((1, d, tn), lambda i, e, j: (e, 0, j))]
    args = [x, wg, wu]
    if gates is not None:
        in_specs.append(pl.BlockSpec((tm, LANES), lambda i, e, j: (i, 0)))
        args.append(gates)
    return pl.pallas_call(
        functools.partial(_swiglu_up_kernel, gated=gates is not None),
        grid=(m // tm, ne, nf), in_specs=in_specs,
        out_specs=pl.BlockSpec((tm, tn), lambda i, e, j: (i, e * nf + j)),
        out_shape=jax.ShapeDtypeStruct((m, ne * f), BF16),
        compiler_params=_params(("parallel", "parallel", "parallel")), name="swiglu_up")(*args)


def _router_kernel(x_ref, g_ref, w_ref, o_ref):
    x = x_ref[...]
    h = x * lax.rsqrt(jnp.mean(x * x, axis=-1, keepdims=True) + NORM_EPS) * g_ref[...]
    logits = lax.dot_general(h, w_ref[...], (((1,), (0,)), ((), ())), precision=HIGHEST,
                             preferred_element_type=F32)
    lane = lax.broadcasted_iota(jnp.int32, logits.shape, 1)
    logits = jnp.where(lane < N_EXPERTS, logits, -jnp.inf)
    v1 = jnp.max(logits, axis=1, keepdims=True)
    i1 = jnp.min(jnp.where(logits == v1, lane, LANES), axis=1, keepdims=True)
    rest = jnp.where(lane == i1, -jnp.inf, logits)
    v2 = jnp.max(rest, axis=1, keepdims=True)
    i2 = jnp.min(jnp.where(rest == v2, lane, LANES), axis=1, keepdims=True)
    e2 = jnp.exp(v2 - v1)
    w1 = 1.0 / (1.0 + e2)
    o_ref[...] = jnp.where(lane == i1, w1, 0.0) + jnp.where(lane == i2, e2 * w1, 0.0)


def moe_gates(x, g, router_pad, tm):
    m, d = x.shape
    return pl.pallas_call(
        _router_kernel, grid=(m // tm,),
        in_specs=[pl.BlockSpec((tm, d), lambda i: (i, 0)), pl.BlockSpec((1, d), lambda i: (0, 0)),
                  pl.BlockSpec((d, LANES), lambda i: (0, 0))],
        out_specs=pl.BlockSpec((tm, LANES), lambda i: (i, 0)),
        out_shape=jax.ShapeDtypeStruct((m, LANES), F32),
        compiler_params=_params(("parallel",)), name="moe_gates")(x, g.reshape(1, d), router_pad)


def _bias_tiles_kernel(table_ref, idx_ref, o_ref):
    h = pl.program_id(0)
    for kind in range(idx_ref.shape[0]):
        idx = idx_ref[kind]
        acc = jnp.zeros(idx.shape, F32)
        for b in range(REL_BUCKETS):
            acc = jnp.where(idx == b, table_ref[b, h], acc)
        o_ref[kind, 0] = acc


def bias_tiles(table, idx):
    kinds, t, _ = idx.shape
    nh = table.shape[1]
    return pl.pallas_call(
        _bias_tiles_kernel, grid=(nh,),
        in_specs=[pl.BlockSpec(memory_space=pltpu.SMEM),
                  pl.BlockSpec((kinds, t, t), lambda h: (0, 0, 0))],
        out_specs=pl.BlockSpec((kinds, 1, t, t), lambda h: (0, h, 0, 0)),
        out_shape=jax.ShapeDtypeStruct((kinds, nh, t, t), F32),
        compiler_params=_params(("parallel",)), name="bias_tiles")(table, idx)


def _rel_bucket(dist):
    n = jnp.maximum(dist, 0)
    max_exact = REL_BUCKETS // 2
    nf = jnp.maximum(n, max_exact).astype(F32)
    large = max_exact + (jnp.log(nf / max_exact) / math.log(REL_MAX_DIST / max_exact)
                         * (REL_BUCKETS - max_exact)).astype(jnp.int32)
    large = jnp.minimum(large, REL_BUCKETS - 1)
    return jnp.where(n < max_exact, n, large)


def _layer_weights(l, w_in, ml_gate_bias, mla_w_uq, mla_q_gain, mla_kr_gain, mla_w_uk, mla_w_uv,
                   diff_q_gain, diff_k_gain, diff_lambda):
    d = w_in.shape[1]
    offs = {}
    o = 0
    for name, width in (("mlq", ML_HEADS * ML_DK), ("mlk", ML_HEADS * ML_DK), ("mlv", ML_W), ("mlo", ML_W),
                        ("i", ML_HEADS), ("f", ML_HEADS), ("qa", MLA_Q_LORA), ("ckv", MLA_KV_LORA),
                        ("kr", MLA_ROPE), ("dq", DIFF_W), ("dk", 2 * DIFF_DH), ("dv", 2 * DIFF_DH),
                        ("gate", 3 * D_MODEL)):
        offs[name] = (o, o + width)
        o += width
    w = w_in[l]
    col = lambda name: w[:, offs[name][0]:offs[name][1]]
    zeros = lambda n: jnp.zeros((d, n), w.dtype)
    w_in_r = jnp.concatenate(
        [col("gate"), col("mlq"), col("mlk"), col("mlv"), col("mlo"), col("dq"), col("qa"), col("ckv"),
         col("dk"), col("dv"), col("kr"), zeros(LANES - MLA_ROPE), col("i"), col("f"),
         zeros(LANES - 2 * ML_HEADS)], axis=1).astype(BF16)
    assert w_in_r.shape[1] == PROJ_COLS

    uq = mla_w_uq[l].reshape(MLA_Q_LORA, MLA_HEADS, MLA_NOPE + MLA_ROPE)
    uq_rope = jnp.pad(uq[:, :, MLA_NOPE:], ((0, 0), (0, 0), (0, LANES - MLA_ROPE)))
    w_uq = jnp.concatenate([uq[:, :, :MLA_NOPE].reshape(MLA_Q_LORA, -1),
                            uq_rope.reshape(MLA_Q_LORA, -1)], axis=1).astype(BF16)
    pad64 = lambda v: jnp.pad(v, (0, LANES - MLA_ROPE)).reshape(1, LANES)
    lq1, lk1, lq2, lk2 = diff_lambda[l]
    lam_init = 0.8 - 0.6 * math.exp(-0.3 * l)
    lam = jnp.exp(jnp.sum(lq1 * lk1)) - jnp.exp(jnp.sum(lq2 * lk2)) + lam_init
    bias_row = jnp.pad(jnp.concatenate([ml_gate_bias[l, 0], ml_gate_bias[l, 1]]),
                       (0, LANES - 2 * ML_HEADS)).reshape(1, LANES)
    seg = jnp.arange(LANES) // DIFF_DH
    return dict(
        w_in=w_in_r, w_uq=w_uq, bias_row=bias_row, bias2=ml_gate_bias[l],
        gq_nope=mla_q_gain[l, :MLA_NOPE].reshape(1, LANES), gq_rope=pad64(mla_q_gain[l, MLA_NOPE:]),
        mla_kr_gain=pad64(mla_kr_gain[l]),
        w_uk_t=jnp.transpose(mla_w_uk[l], (1, 2, 0)).astype(BF16),
        w_uv=jnp.transpose(mla_w_uv[l], (1, 0, 2)).astype(BF16),
        w_uv_t=jnp.transpose(mla_w_uv[l], (1, 2, 0)).astype(BF16),
        diff_q_gain=jnp.tile(diff_q_gain[l], 2).reshape(1, LANES),
        diff_k_gain=jnp.tile(diff_k_gain[l], 2).reshape(1, LANES),
        seg_ones=(seg[:, None] == seg[None, :]).astype(F32),
        lam=lam.reshape(1).astype(F32), lam_init=lam_init)


def _token_mixer(l, x_all, dims, caches, states, page_table, rel_bias_table, wp, norm_mix_l,
                 ml_out_norm_l, mla_q_norm_l, mla_kv_norm_l, diff_out_norm_l, w_branch, w_out_l, tables):
    nb, seq, ns = dims
    mp = nb * seq
    cos, sin, bias_tiles_kq, bias_past_idx = tables
    cache_mla_c, cache_mla_kr, cache_diff_k, cache_diff_v = caches
    state_c, state_n, state_m = states

    m_all = x_all.shape[0]
    tm_big, tm_mid, tm_small = _tile(m_all, 1664, 640, 320), _tile(m_all, 640, 320), _tile(m_all, 320)
    h = rmsnorm_rows(x_all, norm_mix_l, tm_small)
    proj = matmul(h, wp["w_in"][None], 0, tm_big, 512, D_MODEL)

    wp = dict(wp, mla_q_norm=mla_q_norm_l.reshape(1, -1), mla_kv_norm=mla_kv_norm_l.reshape(1, -1))
    ckv, kr, dk, kcat, dkb, dvb, qm, dqn = mixer_prep(proj, cos, sin, wp, PREP_TM)
    dv = proj[:, COL_DV:COL_DV + 2 * DIFF_DH]
    kr = kr[:, :MLA_ROPE]

    hml_p, pc, pn, pm = mlstm_prompt(proj, wp["bias_row"], ml_out_norm_l, nb, seq, ML_CHUNK)
    ps = proj[mp:]
    heads_last = lambda a: jnp.transpose(a.reshape(ns, ML_HEADS, ML_DK), (0, 2, 1))
    qt = heads_last(ps[:, COL_MLQ:COL_MLQ + ML_HEADS * ML_DK])
    kt = heads_last(ps[:, COL_MLK:COL_MLK + ML_HEADS * ML_DK])
    ig = ps[:, COL_IF:COL_IF + ML_HEADS].reshape(ns, 1, ML_HEADS)
    fg = ps[:, COL_IF + ML_HEADS:COL_IF + 2 * ML_HEADS].reshape(ns, 1, ML_HEADS)
    hml_s, sc, sn, sm = mlstm_sample(proj, mp, qt, kt, ig, fg, wp["bias2"], ml_out_norm_l,
                                     state_c, state_n, state_m, l, SAMPLE_G)
    h_ml = jnp.concatenate([hml_p, hml_s.astype(BF16)], axis=0)

    hmla_p = mla_prompt(qm, kcat, wp["w_uv_t"], nb, seq, ATT_T)
    q_s = jnp.transpose(qm[:, mp:], (1, 0, 2))
    hmla_s = mla_decode(page_table, q_s, kcat[mp:].reshape(ns, 1, MLA_KW), wp["w_uv"],
                        cache_mla_c, cache_mla_kr, l, DEC_PAGES)
    h_mla = jnp.concatenate([hmla_p, hmla_s.reshape(ns, MLA_W).astype(BF16)], axis=0)

    out_scale = 1.0 - wp["lam_init"]
    table = rel_bias_table.astype(F32)
    far = table[REL_BUCKETS - 1]
    bias_past = jnp.transpose(table[bias_past_idx])
    hdiff_p = diff_prompt(dqn, dkb, dvb, bias_tiles_kq, far, wp["lam"], diff_out_norm_l, out_scale,
                          nb, seq, ATT_T)
    hdiff_s = diff_decode(page_table, wp["lam"], jnp.transpose(dqn[:, mp:], (1, 0, 2)),
                          dkb[mp:].reshape(ns, 1, LANES), dvb[mp:].reshape(ns, 1, LANES),
                          bias_past[:, -1:], bias_past[:, :-1], diff_out_norm_l, out_scale,
                          cache_diff_k, cache_diff_v, l, DEC_PAGES)
    h_diff = jnp.concatenate([hdiff_p, hdiff_s.reshape(ns, DIFF_W).astype(BF16)], axis=0)

    merged = gated_merge([h_ml, h_mla, h_diff], w_branch, l, proj, tm_mid, 512)
    x_new = matmul(merged, w_out_l, l, tm_mid, 512, D_MODEL, res=x_all)

    split = lambda a: (a[:mp].reshape(nb, seq, -1), a[mp:].reshape(ns, 1, -1))
    ckv_p, ckv_s = split(ckv)
    kr_p, kr_s = split(kr)
    dk_p, dk_s = split(dk)
    dv_p, dv_s = split(dv)
    st_p = (ckv_p, kr_p, dk_p, dv_p, pc, pn.reshape(nb, ML_HEADS, ML_DK), pm[:, :, 0, 0])
    st_s = (ckv_s, kr_s, dk_s, dv_s, sc, sn, sm.reshape(ns, ML_HEADS))
    return x_new, st_p, st_s


def kernel(x_prompt, x_sample, cache_mla_c, cache_mla_kr, cache_diff_k, cache_diff_v, state_mlstm_c, state_mlstm_n, state_mlstm_m, page_table, rel_bias_table, norm_mix, norm_ffn, w_in, ml_gate_bias, ml_out_norm, mla_q_norm, mla_w_uq, mla_q_gain, mla_kv_norm, mla_kr_gain, mla_w_uk, mla_w_uv, diff_q_gain, diff_k_gain, diff_lambda, diff_out_norm, w_ml_o, w_mla_o, w_diff_o, w_out, ffn_w_gate, ffn_w_up, ffn_w_down, moe_router, moe_w_gate, moe_w_up, moe_w_down):
    nb, seq, d = x_prompt.shape
    ns, dec_seq, _ = x_sample.shape
    assert dec_seq == 1 and d == D_MODEL
    depth = w_in.shape[0]
    n_pages = page_table.shape[1]
    past = n_pages * cache_mla_c.shape[2]
    mp = nb * seq
    x_all = jnp.concatenate([x_prompt.reshape(mp, d), x_sample.reshape(ns, d)], axis=0)

    pos = jnp.concatenate([jnp.tile(jnp.arange(seq, dtype=jnp.int32), nb),
                           jnp.full((ns,), past, jnp.int32)])
    inv = ROPE_THETA ** (-jnp.arange(0, MLA_ROPE, 2, dtype=F32) / MLA_ROPE)
    ang = pos.astype(F32)[:, None] * inv[None, :]
    zpad = jnp.zeros((mp + ns, LANES - MLA_ROPE), F32)
    cos = jnp.concatenate([jnp.cos(ang), jnp.cos(ang), zpad], axis=1)
    sin = jnp.concatenate([-jnp.sin(ang), jnp.sin(ang), zpad], axis=1)
    t_i = jnp.arange(ATT_T, dtype=jnp.int32)
    delta = t_i[None, :] - t_i[:, None]
    bias_tiles_idx = jnp.stack([_rel_bucket(delta), _rel_bucket(delta + ATT_T)])
    cache_mla_kr_t = jnp.swapaxes(cache_mla_kr, 2, 3)
    bias_past_idx = _rel_bucket(past - jnp.arange(past + 1, dtype=jnp.int32))
    bias_tiles_kq = bias_tiles(rel_bias_table.astype(F32), bias_tiles_idx)
    tables = (cos, sin, bias_tiles_kq, bias_past_idx)
    w_branch = [w.astype(BF16) for w in (w_ml_o, w_mla_o, w_diff_o)]
    w_out_b = w_out.astype(BF16)
    ffn_w_down_b = ffn_w_down.astype(BF16)

    per_p, per_s = [], []
    for l in range(depth):
        wp = _layer_weights(l, w_in, ml_gate_bias, mla_w_uq, mla_q_gain, mla_kr_gain, mla_w_uk, mla_w_uv,
                            diff_q_gain, diff_k_gain, diff_lambda)
        x_all, st_p, st_s = _token_mixer(
            l, x_all, (nb, seq, ns), (cache_mla_c, cache_mla_kr_t, cache_diff_k, cache_diff_v),
            (state_mlstm_c, state_mlstm_n, state_mlstm_m), page_table, rel_bias_table, wp, norm_mix[l],
            ml_out_norm[l], mla_q_norm[l], mla_kv_norm[l], diff_out_norm[l], w_branch, w_out_b, tables)
        per_p.append(st_p)
        per_s.append(st_s)

        i = l // 2
        m_all = x_all.shape[0]
        tm_big, tm_mid, tm_small = _tile(m_all, 1664, 640, 320), _tile(m_all, 640, 320), _tile(m_all, 320)
        hf = rmsnorm_rows(x_all, norm_ffn[l], tm_small)
        if l % 2 == 0:
            act = swiglu_up(hf, ffn_w_gate[i:i + 1], ffn_w_up[i:i + 1], tm_big, 256)
            f = act.shape[1]
            x_all = matmul(act, ffn_w_down_b, i, tm_mid, 512, f // 2, res=x_all)
        else:
            router_pad = jnp.pad(moe_router[i], ((0, 0), (0, LANES - N_EXPERTS)))
            gates = moe_gates(x_all, norm_ffn[l], router_pad, tm_small)
            act = swiglu_up(hf, moe_w_gate[i], moe_w_up[i], tm_big, 256, gates=gates)
            wd = moe_w_down.reshape(moe_w_down.shape[0], -1, d)
            x_all = matmul(act, wd, i, tm_big, 512, 2048, res=x_all)

    outs_p = [jnp.stack([st[j] for st in per_p]) for j in range(7)]
    outs_s = [jnp.stack([st[j] for st in per_s]) for j in range(7)]
    y_prompt = x_all[:mp].reshape(nb, seq, d)
    y_sample = x_all[mp:].reshape(ns, 1, d)
    return (y_prompt, y_sample, *outs_p, *outs_s)
```
